```python
import math
import jax, jax.numpy as jnp
from jax import lax
import numpy as np

D_MODEL = 1024
BATCH = 8
SEQ = 4096
DEPTH = 2

HEAD_DIM = 64
BLOCK = 128
EPS = 1e-6
NEG_INF = -1e30
SWA_Q_HEADS = 8
SWA_KV_HEADS = 2
SWA_WINDOW = 128
DIL_PATTERNS = ((128, 1), (512, 4), (2048, 16))
N_DIL = 3
DIL_HEADS = 4
SSM_GROUP = 16
SSM_GROUPS = 32
SSM_WIDTH = SSM_GROUP * SSM_GROUPS
SSM_STATE = 64
DT_MIN = 1e-3
DT_MAX = 1e-1
N_BRANCH = 3
FFN_DIM = 2816
CONV_WIDTH = 3

A_Q = SWA_Q_HEADS * HEAD_DIM
A_KV = SWA_KV_HEADS * HEAD_DIM
B_Q = N_DIL * DIL_HEADS * HEAD_DIM
B_KV = DIL_HEADS * HEAD_DIM
GATE_W = N_BRANCH * D_MODEL
IN_SPLITS = (A_Q, A_KV, A_KV, B_Q, B_KV, B_KV, SSM_WIDTH, GATE_W)
IN_WIDTH = A_Q + 2 * A_KV + B_Q + 2 * B_KV + SSM_WIDTH + GATE_W

kernel_name = "hybrid_swa_dilated_s5_gated_block"


def rmsnorm(x, g):
    xf = x.astype(jnp.float32)
    y = xf * lax.rsqrt(jnp.mean(xf * xf, axis=-1, keepdims=True) + EPS)
    return (y * g.astype(jnp.float32)).astype(x.dtype)


def banded_attention(q, k, v, max_offset, sink=None):
    n, L, g, r, dh = q.shape
    nb = -(-L // BLOCK)
    pad = nb * BLOCK - L
    q = jnp.pad(q, ((0, 0), (0, pad), (0, 0), (0, 0), (0, 0)))
    kv_pad = ((0, 0), (BLOCK, pad), (0, 0), (0, 0))
    k = jnp.pad(k, kv_pad).reshape(n, nb + 1, BLOCK, g, dh)
    v = jnp.pad(v, kv_pad).reshape(n, nb + 1, BLOCK, g, dh)
    kk = jnp.concatenate([k[:, :-1], k[:, 1:]], axis=2)
    vv = jnp.concatenate([v[:, :-1], v[:, 1:]], axis=2)
    qb = q.reshape(n, nb, BLOCK, g, r, dh)
    s = jnp.einsum("nbqgrd,nbkgd->nbgrqk", qb, kk).astype(jnp.float32) * (dh ** -0.5)
    qpos = BLOCK + jnp.arange(BLOCK)[:, None]
    kpos = jnp.arange(2 * BLOCK)[None, :]
    off = qpos - kpos
    band = (off >= 0) & (off <= max_offset)
    has_prev = (jnp.arange(nb) > 0)[:, None, None] | (kpos >= BLOCK)[None]
    mask = band[None] & has_prev
    s = jnp.where(mask[None, :, None, None], s, NEG_INF)
    m = jnp.max(s, axis=-1, keepdims=True)
    if sink is not None:
        sk = sink.astype(jnp.float32)[None, None, :, :, None, None]
        m = jnp.maximum(m, sk)
    p = jnp.exp(s - m)
    l = jnp.sum(p, axis=-1, keepdims=True)
    if sink is not None:
        l = l + jnp.exp(sk - m)
    o = jnp.einsum("nbgrqk,nbkgd->nbgrqd", p.astype(vv.dtype), vv).astype(jnp.float32) / l
    o = jnp.transpose(o, (0, 1, 4, 2, 3, 5)).reshape(n, nb * BLOCK, g, r, dh)[:, :L]
    lse = jnp.transpose((m + jnp.log(l))[..., 0], (0, 1, 4, 2, 3)).reshape(n, nb * BLOCK, g, r)[:, :L]
    return o.astype(q.dtype), lse


def to_sub(x, dil):
    b, s = x.shape[:2]
    rest = x.shape[2:]
    x = x.reshape((b, s // dil, dil) + rest)
    return jnp.moveaxis(x, 2, 1).reshape((b * dil, s // dil) + rest)


def from_sub(x, b, dil):
    n, L = x.shape[:2]
    rest = x.shape[2:]
    x = x.reshape((b, dil, L) + rest)
    return jnp.moveaxis(x, 1, 2).reshape((b, L * dil) + rest)


def dilated_attention(q, k, v):
    bsz, s = q.shape[:2]
    outs, lses = [], []
    for gi, (window, dil) in enumerate(DIL_PATTERNS):
        o, lse = banded_attention(to_sub(q[:, :, gi], dil)[:, :, :, None], to_sub(k, dil), to_sub(v, dil),
                                  window // dil)
        outs.append(from_sub(o[:, :, :, 0], bsz, dil))
        lses.append(from_sub(lse[..., 0], bsz, dil))
    wts = jax.nn.softmax(jnp.stack(lses), axis=0)
    y = jnp.sum(wts[..., None] * jnp.stack(outs).astype(jnp.float32), axis=0)
    return y.reshape(bsz, s, B_KV).astype(q.dtype)


def s5_mixer(u, lam_re, lam_im, log_dt, b_re, b_im, c_re, c_im, d_skip, w_glu, b_glu):
    f32 = jnp.float32
    bsz, s, _ = u.shape
    uf = u.astype(f32).reshape(bsz, s, SSM_GROUPS, SSM_GROUP)
    lr, li = lam_re.astype(f32), lam_im.astype(f32)
    dt = jnp.exp(log_dt.astype(f32))[:, None]
    mag = jnp.exp(lr * dt)
    ab_re, ab_im = mag * jnp.cos(li * dt), mag * jnp.sin(li * dt)
    nr, ni = ab_re - 1.0, ab_im
    den = lr * lr + li * li
    f_re = (nr * lr + ni * li) / den
    f_im = (ni * lr - nr * li) / den
    br, bi = b_re.astype(f32), b_im.astype(f32)
    bb_re = f_re[..., None] * br - f_im[..., None] * bi
    bb_im = f_re[..., None] * bi + f_im[..., None] * br
    bu_re = jnp.einsum("bsgh,gph->bsgp", uf, bb_re)
    bu_im = jnp.einsum("bsgh,gph->bsgp", uf, bb_im)
    a_re = jnp.broadcast_to(ab_re, bu_re.shape)
    a_im = jnp.broadcast_to(ab_im, bu_im.shape)

    def combine(e1, e2):
        a1r, a1i, b1r, b1i = e1
        a2r, a2i, b2r, b2i = e2
        return (a2r * a1r - a2i * a1i, a2r * a1i + a2i * a1r,
                a2r * b1r - a2i * b1i + b2r, a2r * b1i + a2i * b1r + b2i)

    _, _, xr, xi = lax.associative_scan(combine, (a_re, a_im, bu_re, bu_im), axis=1)
    y = (jnp.einsum("bsgp,ghp->bsgh", xr, c_re.astype(f32))
         - jnp.einsum("bsgp,ghp->bsgh", xi, c_im.astype(f32))
         + d_skip.astype(f32).reshape(SSM_GROUPS, SSM_GROUP) * uf)
    z = jax.nn.gelu(y.reshape(bsz, s, SSM_WIDTH))
    z = z * jax.nn.sigmoid(z @ w_glu.astype(f32) + b_glu.astype(f32))
    return z.astype(u.dtype)


def hybrid_mixer(h, w_in, attn_sinks, lam_re, lam_im, log_dt, b_re, b_im, c_re, c_im, d_skip, w_glu, b_glu,
                 w_branch_a, w_branch_b, w_branch_c, w_out):
    bsz, s, _ = h.shape
    proj = h @ w_in
    cuts = [int(c) for c in np.cumsum(IN_SPLITS)[:-1]]
    qa, ka, va, qd, kd, vd, u, g = jnp.split(proj, cuts, axis=-1)
    rep = SWA_Q_HEADS // SWA_KV_HEADS
    ya, _ = banded_attention(qa.reshape(bsz, s, SWA_KV_HEADS, rep, HEAD_DIM),
                             ka.reshape(bsz, s, SWA_KV_HEADS, HEAD_DIM),
                             va.reshape(bsz, s, SWA_KV_HEADS, HEAD_DIM),
                             SWA_WINDOW - 1, attn_sinks.reshape(SWA_KV_HEADS, rep))
    ya = ya.reshape(bsz, s, A_Q)
    yb = dilated_attention(qd.reshape(bsz, s, N_DIL, DIL_HEADS, HEAD_DIM),
                           kd.reshape(bsz, s, DIL_HEADS, HEAD_DIM),
                           vd.reshape(bsz, s, DIL_HEADS, HEAD_DIM))
    yc = s5_mixer(u, lam_re, lam_im, log_dt, b_re, b_im, c_re, c_im, d_skip, w_glu, b_glu)
    gates = jax.nn.sigmoid(g.astype(jnp.float32)).reshape(bsz, s, N_BRANCH, D_MODEL)
    merged = (gates[:, :, 0] * (ya @ w_branch_a).astype(jnp.float32)
              + gates[:, :, 1] * (yb @ w_branch_b).astype(jnp.float32)
              + gates[:, :, 2] * (yc @ w_branch_c).astype(jnp.float32))
    return merged.astype(h.dtype) @ w_out


def conv_ffn(h, w_up, conv_w, conv_b, w_down):
    up = h @ w_up
    up = lax.conv_general_dilated(up, conv_w[:, None, :], window_strides=(1,),
                                  padding=[(CONV_WIDTH - 1, 0)],
                                  dimension_numbers=("NWC", "WIO", "NWC"),
                                  feature_group_count=2 * FFN_DIM) + conv_b
    gate, val = jnp.split(up, 2, axis=-1)
    return (jax.nn.silu(gate) * val) @ w_down


def setup_inputs(seed: int = 0) -> dict:
    key = jax.random.key(seed)
    ks = jax.random.split(key, 26)
    f32 = jnp.float32
    L = DEPTH

    def nrm(k, shape, scale):
        return jax.random.normal(k, shape, f32) * scale

    lam_im = jnp.broadcast_to(jnp.pi * jnp.arange(SSM_STATE, dtype=f32), (L, SSM_GROUPS, SSM_STATE))
    return {
        "x": nrm(ks[0], (BATCH, SEQ, D_MODEL), 1.0),
        "norm_mix": 1.0 + nrm(ks[1], (L, D_MODEL), 0.02),
        "w_in": nrm(ks[2], (L, D_MODEL, IN_WIDTH), D_MODEL ** -0.5),
        "attn_sinks": nrm(ks[3], (L, SWA_Q_HEADS), 0.5),
        "ssm_lambda_re": -0.5 * jnp.exp(nrm(ks[4], (L, SSM_GROUPS, SSM_STATE), 0.05)),
        "ssm_lambda_im": lam_im + nrm(ks[5], (L, SSM_GROUPS, SSM_STATE), 0.01),
        "ssm_log_dt": jax.random.uniform(ks[6], (L, SSM_GROUPS), f32, math.log(DT_MIN), math.log(DT_MAX)),
        "ssm_b_re": nrm(ks[7], (L, SSM_GROUPS, SSM_STATE, SSM_GROUP), (2 * SSM_GROUP) ** -0.5),
        "ssm_b_im": nrm(ks[8], (L, SSM_GROUPS, SSM_STATE, SSM_GROUP), (2 * SSM_GROUP) ** -0.5),
        "ssm_c_re": nrm(ks[9], (L, SSM_GROUPS, SSM_GROUP, SSM_STATE), (2 * SSM_STATE) ** -0.5),
        "ssm_c_im": nrm(ks[10], (L, SSM_GROUPS, SSM_GROUP, SSM_STATE), (2 * SSM_STATE) ** -0.5),
        "ssm_d": nrm(ks[11], (L, SSM_WIDTH), 1.0),
        "w_glu": nrm(ks[12], (L, SSM_WIDTH, SSM_WIDTH), SSM_WIDTH ** -0.5),
        "b_glu": nrm(ks[13], (L, SSM_WIDTH), 0.02),
        "w_branch_a": nrm(ks[14], (L, A_Q, D_MODEL), A_Q ** -0.5),
        "w_branch_b": nrm(ks[15], (L, B_KV, D_MODEL), B_KV ** -0.5),
        "w_branch_c": nrm(ks[16], (L, SSM_WIDTH, D_MODEL), SSM_WIDTH ** -0.5),
        "w_out": nrm(ks[17], (L, D_MODEL, D_MODEL), D_MODEL ** -0.5),
        "norm_ffn": 1.0 + nrm(ks[18], (L, D_MODEL), 0.02),
        "w_up": nrm(ks[19], (L, D_MODEL, 2 * FFN_DIM), D_MODEL ** -0.5),
        "conv_w": nrm(ks[20], (L, CONV_WIDTH, 2 * FFN_DIM), CONV_WIDTH ** -0.5),
        "conv_b": nrm(ks[21], (L, 2 * FFN_DIM), 0.02),
        "w_down": nrm(ks[22], (L, FFN_DIM, D_MODEL), FFN_DIM ** -0.5),
        "norm_final": 1.0 + nrm(ks[23], (D_MODEL,), 0.02),
    }


def reference(x, norm_mix, w_in, attn_sinks, ssm_lambda_re, ssm_lambda_im, ssm_log_dt, ssm_b_re, ssm_b_im,
              ssm_c_re, ssm_c_im, ssm_d, w_glu, b_glu, w_branch_a, w_branch_b, w_branch_c, w_out,
              norm_ffn, w_up, conv_w, conv_b, w_down, norm_final):
    for l in range(DEPTH):
        h = rmsnorm(x, norm_mix[l])
        x = x + hybrid_mixer(h, w_in[l], attn_sinks[l], ssm_lambda_re[l], ssm_lambda_im[l], ssm_log_dt[l],
                             ssm_b_re[l], ssm_b_im[l], ssm_c_re[l], ssm_c_im[l], ssm_d[l], w_glu[l], b_glu[l],
                             w_branch_a[l], w_branch_b[l], w_branch_c[l], w_out[l]).astype(x.dtype)
        h = rmsnorm(x, norm_ffn[l])
        x = x + conv_ffn(h, w_up[l], conv_w[l], conv_b[l], w_down[l]).astype(x.dtype)
    return rmsnorm(x, norm_final)
```

```python
import functools

import jax
import jax.numpy as jnp
from jax import lax
from jax.experimental import pallas as pl
from jax.experimental.pallas import tpu as pltpu

F32 = jnp.float32
BF16 = jnp.bfloat16

D_MODEL = 1024
HEAD_DIM = 64
BLOCK = 128
EPS = 1e-6
NEG_INF = -1e30
SWA_Q_HEADS = 8
SWA_KV_HEADS = 2
SWA_WINDOW = 128
DIL_PATTERNS = ((128, 1), (512, 4), (2048, 16))
N_DIL = 3
DIL_HEADS = 4
SSM_GROUP = 16
SSM_GROUPS = 32
SSM_WIDTH = SSM_GROUP * SSM_GROUPS
SSM_STATE = 64
N_STATE = SSM_GROUPS * SSM_STATE
N_BRANCH = 3
FFN_DIM = 2816
CONV_WIDTH = 3

A_Q = SWA_Q_HEADS * HEAD_DIM
A_KV = SWA_KV_HEADS * HEAD_DIM
B_Q = N_DIL * DIL_HEADS * HEAD_DIM
B_KV = DIL_HEADS * HEAD_DIM
GATE_W = N_BRANCH * D_MODEL
W_MAIN = A_Q + 2 * A_KV + B_Q + 2 * B_KV + SSM_WIDTH

OFF_QA, OFF_KA, OFF_VA = 0, A_Q, A_Q + A_KV
OFF_QD = A_Q + 2 * A_KV
OFF_KD = OFF_QD + B_Q
OFF_VD = OFF_KD + B_KV
OFF_U = OFF_VD + B_KV

SUBLANES_V7X = 8
MIB = 1024 * 1024

TOK_TILE = 512
ATTN_TILE = 512
SSM_T = 64
SSM_CHUNK = 512
FFN_CHUNK = 256


def _cparams(semantics, vmem_mib):
    return pltpu.CompilerParams(dimension_semantics=semantics, vmem_limit_bytes=vmem_mib * MIB)


def _resident(shape):
    nd = len(shape)
    return pl.BlockSpec(shape, lambda *_: (0,) * nd, pipeline_mode=pl.Buffered(1))


def _rmsnorm(x, g):
    return x * lax.rsqrt(jnp.mean(x * x, axis=-1, keepdims=True) + EPS) * g


def _dot(a, b):
    return jnp.dot(a, b, preferred_element_type=F32)


def _inproj_body(x_ref, g_ref, w_ref, o_ref):
    h = _rmsnorm(x_ref[...], g_ref[...]).astype(BF16)
    step = 512
    for c in range(W_MAIN // step):
        o_ref[:, c * step:(c + 1) * step] = _dot(h, w_ref[:, c * step:(c + 1) * step]).astype(BF16)


def _inproj(x, gain, w_main):
    bsz, s, _ = x.shape
    return pl.pallas_call(
        _inproj_body,
        grid=(bsz, s // TOK_TILE),
        in_specs=[pl.BlockSpec((None, TOK_TILE, D_MODEL), lambda b, i: (b, i, 0)),
                  _resident((1, D_MODEL)),
                  _resident((D_MODEL, W_MAIN))],
        out_specs=pl.BlockSpec((None, TOK_TILE, W_MAIN), lambda b, i: (b, i, 0)),
        out_shape=jax.ShapeDtypeStruct((bsz, s, W_MAIN), BF16),
        compiler_params=_cparams(("parallel", "parallel"), 40),
        name="inproj",
    )(x, gain, w_main)


def _attn_body(*refs, n_kv, rep, max_off, tq, use_sink, want_lse):
    q_ref, kp_ref, k_ref, vp_ref, v_ref = refs[:5]
    pos = 5
    sink_ref = None
    if use_sink:
        sink_ref = refs[pos]
        pos += 1
    o_ref = refs[pos]
    lse_ref = refs[pos + 1] if want_lse else None

    i = pl.program_id(1)
    rows = lax.broadcasted_iota(jnp.int32, (BLOCK, 2 * BLOCK), 0)
    cols = lax.broadcasted_iota(jnp.int32, (BLOCK, 2 * BLOCK), 1)
    hi = rows + BLOCK
    lo = hi - max_off
    band_cap = jnp.where(cols <= hi, jnp.where(cols >= lo, jnp.inf, NEG_INF), NEG_INF).astype(F32)
    first_lo = jnp.where(i > 0, 0, BLOCK)
    first_cap = jnp.where(cols >= first_lo, band_cap, NEG_INF)

    for j in range(tq // BLOCK):
        cap = first_cap if j == 0 else band_cap
        qrows = slice(j * BLOCK, (j + 1) * BLOCK)
        for g in range(n_kv):
            kcols = slice(g * HEAD_DIM, (g + 1) * HEAD_DIM)
            if j == 0:
                k_prev, v_prev = kp_ref[:, kcols], vp_ref[:, kcols]
            else:
                prows = slice((j - 1) * BLOCK, j * BLOCK)
                k_prev, v_prev = k_ref[prows, kcols], v_ref[prows, kcols]
            kk = jnp.concatenate([k_prev, k_ref[qrows, kcols]], axis=0)
            vv = jnp.concatenate([v_prev, v_ref[qrows, kcols]], axis=0)
            heads = [g * rep + r for r in range(rep)]
            qs = jnp.concatenate(
                [q_ref[qrows, h * HEAD_DIM:(h + 1) * HEAD_DIM] for h in heads], axis=0)
            qs = qs * (HEAD_DIM ** -0.5)
            s = lax.dot_general(qs, kk, (((1,), (1,)), ((), ())), preferred_element_type=F32)
            s = jnp.minimum(s.reshape(rep, BLOCK, 2 * BLOCK), cap[None])
            m = jnp.max(s, axis=-1, keepdims=True)
            if use_sink:
                sk = jnp.stack([jnp.full((BLOCK, 1), sink_ref[h], F32) for h in heads], axis=0)
                m = jnp.maximum(m, sk)
            p = jnp.exp(s - m)
            l = jnp.sum(p, axis=-1, keepdims=True)
            if use_sink:
                l = l + jnp.exp(sk - m)
            pv = _dot(p.reshape(rep * BLOCK, 2 * BLOCK).astype(BF16), vv)
            o = pv.reshape(rep, BLOCK, HEAD_DIM) / l
            for r, h in enumerate(heads):
                hcols = slice(h * HEAD_DIM, (h + 1) * HEAD_DIM)
                o_ref[qrows, hcols] = o[r].astype(o_ref.dtype)
                if want_lse:
                    lse_ref[qrows, hcols] = jnp.broadcast_to(m[r] + jnp.log(l[r]), (BLOCK, HEAD_DIM))


def _banded_attention(q_arr, q_blk, k_arr, k_blk, v_arr, v_blk, *, n_kv, rep, max_off, sinks=None,
                      want_lse=False):
    n, seq_len, _ = q_arr.shape
    tq = min(ATTN_TILE, seq_len)
    per = tq // BLOCK
    qw = n_kv * rep * HEAD_DIM
    kw = n_kv * HEAD_DIM
    cur = lambda blk: (lambda b, i: (b, i, blk))
    prev = lambda blk: (lambda b, i: (b, jnp.maximum(i * per - 1, 0), blk))
    in_specs = [pl.BlockSpec((None, tq, qw), cur(q_blk)),
                pl.BlockSpec((None, BLOCK, kw), prev(k_blk)),
                pl.BlockSpec((None, tq, kw), cur(k_blk)),
                pl.BlockSpec((None, BLOCK, kw), prev(v_blk)),
                pl.BlockSpec((None, tq, kw), cur(v_blk))]
    args = [q_arr, k_arr, k_arr, v_arr, v_arr]
    if sinks is not None:
        in_specs.append(pl.BlockSpec(memory_space=pltpu.SMEM))
        args.append(sinks)
    out_spec = pl.BlockSpec((None, tq, qw), lambda b, i: (b, i, 0))
    out_shape = [jax.ShapeDtypeStruct((n, seq_len, qw), BF16)]
    out_specs = [out_spec]
    if want_lse:
        out_shape.append(jax.ShapeDtypeStruct((n, seq_len, qw), F32))
        out_specs.append(out_spec)
    body = functools.partial(_attn_body, n_kv=n_kv, rep=rep, max_off=max_off, tq=tq,
                             use_sink=sinks is not None, want_lse=want_lse)
    res = pl.pallas_call(
        body,
        grid=(n, seq_len // tq),
        in_specs=in_specs,
        out_specs=out_specs,
        out_shape=out_shape,
        compiler_params=_cparams(("parallel", "parallel"), 32),
        name="banded_attention",
    )(*args)
    return res if want_lse else res[0]


def _ssm_param_body(lr_ref, li_ref, ldt_ref, br_ref, bi_ref, ar_ref, ai_ref, bbr_ref, bbi_ref):
    lr, li = lr_ref[...], li_ref[...]
    dt = jnp.exp(ldt_ref[...])
    mag = jnp.exp(lr * dt)
    ab_re, ab_im = mag * jnp.cos(li * dt), mag * jnp.sin(li * dt)
    nr, ni = ab_re - 1.0, ab_im
    den = lr * lr + li * li
    f_re = (nr * lr + ni * li) / den
    f_im = (ni * lr - nr * li) / den
    br, bi = br_ref[...], bi_ref[...]
    ar_ref[...] = ab_re
    ai_ref[...] = ab_im
    bbr_ref[...] = f_re * br - f_im * bi
    bbi_ref[...] = f_re * bi + f_im * br


def _ssm_params(lam_re, lam_im, log_dt, b_re, b_im):
    nl = lam_re.shape[0]
    rows = nl * SSM_GROUPS * SSM_GROUP
    rep = lambda a: jnp.broadcast_to(a[:, :, None, :], (nl, SSM_GROUPS, SSM_GROUP, SSM_STATE)).reshape(
        rows, SSM_STATE)
    ldt = jnp.broadcast_to(log_dt[:, :, None, None], (nl, SSM_GROUPS, SSM_GROUP, SSM_STATE)).reshape(
        rows, SSM_STATE)
    bt = lambda b: jnp.swapaxes(b, 2, 3).reshape(rows, SSM_STATE)
    shape = jax.ShapeDtypeStruct((rows, SSM_STATE), F32)
    ar, ai, bbr, bbi = pl.pallas_call(
        _ssm_param_body, out_shape=[shape] * 4, name="ssm_params",
    )(rep(lam_re), rep(lam_im), ldt, bt(b_re), bt(b_im))
    pick = lambda a: a.reshape(nl, SSM_GROUPS, SSM_GROUP, SSM_STATE)[:, :, 0, :].reshape(nl, N_STATE)
    unflat = lambda a: a.reshape(nl, SSM_GROUPS, SSM_GROUP, SSM_STATE)
    return pick(ar), pick(ai), unflat(bbr), unflat(bbi)


SSM_HALF_GROUPS = SSM_GROUPS // 2
SSM_HALF_IN = SSM_HALF_GROUPS * SSM_GROUP
SSM_HALF_STATE = SSM_HALF_GROUPS * SSM_STATE


def _ssm_matrices(bb_re, bb_im, c_re, c_im):
    eye = jnp.eye(SSM_HALF_GROUPS, dtype=F32)

    def in_blocks(bb):
        bb = bb.reshape(2, SSM_HALF_GROUPS, SSM_GROUP, SSM_STATE)
        return jnp.einsum("ab,kahp->kahbp", eye, bb).reshape(2, SSM_HALF_IN, SSM_HALF_STATE)

    def out_blocks(c):
        c = c.reshape(2, SSM_HALF_GROUPS, SSM_GROUP, SSM_STATE)
        return jnp.einsum("ab,kahp->kapbh", eye, c).reshape(2, SSM_HALF_STATE, SSM_HALF_IN)

    b_mat = jnp.concatenate([in_blocks(bb_re), in_blocks(bb_im)], axis=2).astype(BF16)
    c_mat = jnp.concatenate([out_blocks(c_re), out_blocks(-c_im)], axis=1).astype(BF16)
    return b_mat, c_mat


def _ssm_body(u_ref, bm_ref, ar_ref, ai_ref, cm_ref, d_ref, wg_ref, bg_ref, o_ref,
              x_scr, sr_scr, si_scr):
    nb = SUBLANES_V7X
    width = 2 * SSM_HALF_STATE

    @pl.when(pl.program_id(0) == 0)
    def _():
        sr_scr[...] = jnp.zeros_like(sr_scr)
        si_scr[...] = jnp.zeros_like(si_scr)

    u = u_ref[...]
    for half in range(2):
        ucols = slice(half * SSM_HALF_IN, (half + 1) * SSM_HALF_IN)
        x_scr[:, half * width:(half + 1) * width] = _dot(u[:, ucols], bm_ref[half])

    for half in range(2):
        for c in range(SSM_HALF_STATE // SSM_CHUNK):
            re0 = half * width + c * SSM_CHUNK
            im0 = re0 + SSM_HALF_STATE
            st0 = half * SSM_HALF_STATE + c * SSM_CHUNK
            a_re = ar_ref[:, st0:st0 + SSM_CHUNK]
            a_im = ai_ref[:, st0:st0 + SSM_CHUNK]

            def step(t, carry, re0=re0, im0=im0, a_re=a_re, a_im=a_im):
                xr, xi = carry
                rows = pl.ds(pl.multiple_of(t * nb, nb), nb)
                nr = a_re * xr - a_im * xi + x_scr[rows, re0:re0 + SSM_CHUNK]
                ni = a_re * xi + a_im * xr + x_scr[rows, im0:im0 + SSM_CHUNK]
                x_scr[rows, re0:re0 + SSM_CHUNK] = nr
                x_scr[rows, im0:im0 + SSM_CHUNK] = ni
                return nr, ni

            xr, xi = lax.fori_loop(0, SSM_T, step,
                                   (sr_scr[:, st0:st0 + SSM_CHUNK], si_scr[:, st0:st0 + SSM_CHUNK]),
                                   unroll=8)
            sr_scr[:, st0:st0 + SSM_CHUNK] = xr
            si_scr[:, st0:st0 + SSM_CHUNK] = xi

    ys = [_dot(x_scr[:, half * width:(half + 1) * width].astype(BF16), cm_ref[half]) for half in range(2)]
    y = jnp.concatenate(ys, axis=1) + d_ref[...] * u.astype(F32)
    z = jax.nn.gelu(y)
    z = z * jax.nn.sigmoid(_dot(z.astype(BF16), wg_ref[...]) + bg_ref[...])
    o_ref[...] = z.astype(o_ref.dtype)


def _ssm(u_tm, b_mat, a_re, a_im, c_mat, d_skip, w_glu, b_glu):
    rows_total = u_tm.shape[0]
    rows = SSM_T * SUBLANES_V7X
    return pl.pallas_call(
        _ssm_body,
        grid=(rows_total // rows,),
        in_specs=[pl.BlockSpec((rows, SSM_WIDTH), lambda i: (i, 0)),
                  _resident(b_mat.shape), _resident(a_re.shape), _resident(a_im.shape),
                  _resident(c_mat.shape), _resident(d_skip.shape), _resident(w_glu.shape),
                  _resident(b_glu.shape)],
        out_specs=pl.BlockSpec((rows, SSM_WIDTH), lambda i: (i, 0)),
        out_shape=jax.ShapeDtypeStruct((rows_total, SSM_WIDTH), BF16),
        scratch_shapes=[pltpu.VMEM((rows, 2 * N_STATE), F32),
                        pltpu.VMEM((SUBLANES_V7X, N_STATE), F32),
                        pltpu.VMEM((SUBLANES_V7X, N_STATE), F32)],
        compiler_params=_cparams(("arbitrary",), 40),
        name="s5_mixer",
    )(u_tm, b_mat, a_re, a_im, c_mat, d_skip, w_glu, b_glu)


def _merge_body(x_ref, ya_ref, o0_ref, o1_ref, o2_ref, l0_ref, l1_ref, l2_ref, yc_ref,
                gn_ref, wg_ref, wa_ref, wb_ref, wc_ref, wo_ref, out_ref, merged_scr):
    x = x_ref[...]
    h = _rmsnorm(x, gn_ref[...]).astype(BF16)
    l0, l1, l2 = l0_ref[...], l1_ref[...], l2_ref[...]
    top = jnp.maximum(jnp.maximum(l0, l1), l2)
    e0, e1, e2 = jnp.exp(l0 - top), jnp.exp(l1 - top), jnp.exp(l2 - top)
    inv = 1.0 / (e0 + e1 + e2)
    yb = ((e0 * inv) * o0_ref[...].astype(F32) + (e1 * inv) * o1_ref[...].astype(F32)
          + (e2 * inv) * o2_ref[...].astype(F32)).astype(BF16)
    ya = ya_ref[...]
    yc = yc_ref[...]
    step = 256
    for c in range(D_MODEL // step):
        cs = slice(c * step, (c + 1) * step)
        gate = lambda k: jax.nn.sigmoid(_dot(h, wg_ref[:, k * D_MODEL + c * step:k * D_MODEL + (c + 1) * step]))
        merged = (gate(0) * _dot(ya, wa_ref[:, cs]) + gate(1) * _dot(yb, wb_ref[:, cs])
                  + gate(2) * _dot(yc, wc_ref[:, cs]))
        merged_scr[:, cs] = merged.astype(BF16)
    out_ref[...] = x + _dot(merged_scr[...], wo_ref[...])


def _merge(x, ya, outs, lses, yc, gain, w_gate, w_a, w_b, w_c, w_out):
    bsz, s, _ = x.shape
    tile = lambda width: pl.BlockSpec((None, TOK_TILE, width), lambda b, i: (b, i, 0))
    return pl.pallas_call(
        _merge_body,
        grid=(bsz, s // TOK_TILE),
        in_specs=[tile(D_MODEL), tile(A_Q)] + [tile(B_KV)] * 6 + [tile(SSM_WIDTH),
                  _resident(gain.shape), _resident(w_gate.shape), _resident(w_a.shape),
                  _resident(w_b.shape), _resident(w_c.shape), _resident(w_out.shape)],
        out_specs=tile(D_MODEL),
        out_shape=jax.ShapeDtypeStruct(x.shape, x.dtype),
        scratch_shapes=[pltpu.VMEM((TOK_TILE, D_MODEL), BF16)],
        compiler_params=_cparams(("parallel", "parallel"), 48),
        name="gated_merge",
    )(x, ya, *outs, *lses, yc, gain, w_gate, w_a, w_b, w_c, w_out)


def _ffn_body(*refs, final):
    x_ref, gn_ref, wup_ref, cw_ref, cb_ref, wdn_ref = refs[:6]
    pos = 6
    gf_ref = None
    if final:
        gf_ref = refs[pos]
        pos += 1
    out_ref, carry_scr, work_scr, act_scr = refs[pos:pos + 4]
    pad = SUBLANES_V7X

    @pl.when(pl.program_id(1) == 0)
    def _():
        carry_scr[...] = jnp.zeros_like(carry_scr)

    x = x_ref[...]
    h = _rmsnorm(x, gn_ref[...]).astype(BF16)

    def conv_cols(slot, c0):
        cs = slice(c0, c0 + FFN_CHUNK)
        up = _dot(h, wup_ref[:, cs])
        work_scr[slot, 0:pad, :] = carry_scr[:, cs]
        work_scr[slot, pad:pad + TOK_TILE, :] = up
        carry_scr[:, cs] = up[TOK_TILE - pad:, :]
        acc = cb_ref[:, cs] + cw_ref[CONV_WIDTH - 1:CONV_WIDTH, cs] * up
        for k in range(CONV_WIDTH - 1):
            shift = CONV_WIDTH - 1 - k
            acc = acc + cw_ref[k:k + 1, cs] * work_scr[slot, pad - shift:pad - shift + TOK_TILE, :]
        return acc

    for c in range(FFN_DIM // FFN_CHUNK):
        gate = conv_cols(0, c * FFN_CHUNK)
        val = conv_cols(1, FFN_DIM + c * FFN_CHUNK)
        act_scr[:, c * FFN_CHUNK:(c + 1) * FFN_CHUNK] = (jax.nn.silu(gate) * val).astype(BF16)
    y = x + _dot(act_scr[...], wdn_ref[...])
    if final:
        y = _rmsnorm(y, gf_ref[...])
    out_ref[...] = y


def _ffn(x, gain, w_up, conv_w, conv_b, w_down, final_gain=None):
    bsz, s, _ = x.shape
    final = final_gain is not None
    tile = pl.BlockSpec((None, TOK_TILE, D_MODEL), lambda b, i: (b, i, 0))
    in_specs = [tile, _resident(gain.shape), _resident(w_up.shape), _resident(conv_w.shape),
                _resident(conv_b.shape), _resident(w_down.shape)]
    args = [x, gain, w_up, conv_w, conv_b, w_down]
    if final:
        in_specs.append(_resident(final_gain.shape))
        args.append(final_gain)
    return pl.pallas_call(
        functools.partial(_ffn_body, final=final),
        grid=(bsz, s // TOK_TILE),
        in_specs=in_specs,
        out_specs=tile,
        out_shape=jax.ShapeDtypeStruct(x.shape, x.dtype),
        scratch_shapes=[pltpu.VMEM((SUBLANES_V7X, 2 * FFN_DIM), F32),
                        pltpu.VMEM((2, SUBLANES_V7X + TOK_TILE, FFN_CHUNK), F32),
                        pltpu.VMEM((TOK_TILE, FFN_DIM), BF16)],
        compiler_params=_cparams(("arbitrary", "arbitrary"), 56),
        name="conv_ffn",
    )(*args)


def _to_sub(a, dil):
    b, s, c = a.shape
    return a.reshape(b, s // dil, dil, c).transpose(0, 2, 1, 3).reshape(b * dil, s // dil, c)


def _from_sub(a, b, dil):
    n, length, c = a.shape
    return a.reshape(b, dil, length, c).transpose(0, 2, 1, 3).reshape(b, length * dil, c)


def kernel(x, norm_mix, w_in, attn_sinks, ssm_lambda_re, ssm_lambda_im, ssm_log_dt, ssm_b_re, ssm_b_im,
           ssm_c_re, ssm_c_im, ssm_d, w_glu, b_glu, w_branch_a, w_branch_b, w_branch_c, w_out,
           norm_ffn, w_up, conv_w, conv_b, w_down, norm_final):
    bsz, s, _ = x.shape
    depth = w_in.shape[0]
    assert bsz == SUBLANES_V7X, "the S5 kernel maps the batch onto the sublane axis"
    assert s % TOK_TILE == 0 and s % SSM_T == 0
    assert all((s // dil) % BLOCK == 0 for _, dil in DIL_PATTERNS)

    a_re, a_im, bb_re, bb_im = _ssm_params(ssm_lambda_re, ssm_lambda_im, ssm_log_dt, ssm_b_re, ssm_b_im)
    row = lambda v: v.reshape(1, -1)
    kv_w = B_KV

    for l in range(depth):
        w_main = w_in[l, :, :W_MAIN].astype(BF16)
        w_gate = w_in[l, :, W_MAIN:].astype(BF16)
        proj = _inproj(x, row(norm_mix[l]), w_main)

        ya = _banded_attention(proj, OFF_QA // A_Q, proj, OFF_KA // A_KV, proj, OFF_VA // A_KV,
                               n_kv=SWA_KV_HEADS, rep=SWA_Q_HEADS // SWA_KV_HEADS,
                               max_off=SWA_WINDOW - 1, sinks=attn_sinks[l])

        outs, lses = [], []
        for gi, (window, dil) in enumerate(DIL_PATTERNS):
            q0 = OFF_QD + gi * kv_w
            if dil == 1:
                o, lse = _banded_attention(proj, q0 // kv_w, proj, OFF_KD // kv_w, proj, OFF_VD // kv_w,
                                           n_kv=DIL_HEADS, rep=1, max_off=window // dil, want_lse=True)
            else:
                qkv = jnp.concatenate([proj[:, :, q0:q0 + kv_w], proj[:, :, OFF_KD:OFF_KD + 2 * kv_w]], axis=-1)
                sub = _to_sub(qkv, dil)
                o, lse = _banded_attention(sub, 0, sub, 1, sub, 2, n_kv=DIL_HEADS, rep=1,
                                           max_off=window // dil, want_lse=True)
                o, lse = _from_sub(o, bsz, dil), _from_sub(lse, bsz, dil)
            outs.append(o)
            lses.append(lse)

        u_tm = proj[:, :, OFF_U:].transpose(1, 0, 2).reshape(s * bsz, SSM_WIDTH)
        b_mat, c_mat = _ssm_matrices(bb_re[l], bb_im[l], ssm_c_re[l], ssm_c_im[l])
        bcast = lambda v: jnp.broadcast_to(v[None, :], (SUBLANES_V7X, N_STATE))
        yc_tm = _ssm(u_tm, b_mat, bcast(a_re[l]), bcast(a_im[l]), c_mat, row(ssm_d[l]),
                     w_glu[l].astype(BF16), row(b_glu[l]))
        yc = yc_tm.reshape(s, bsz, SSM_WIDTH).transpose(1, 0, 2)

        x = _merge(x, ya, outs, lses, yc, row(norm_mix[l]), w_gate, w_branch_a[l].astype(BF16),
                   w_branch_b[l].astype(BF16), w_branch_c[l].astype(BF16), w_out[l].astype(BF16))
        x = _ffn(x, row(norm_ffn[l]), w_up[l].astype(BF16), conv_w[l], row(conv_b[l]),
                 w_down[l].astype(BF16), final_gain=row(norm_final) if l == depth - 1 else None)
    return x
```

```python
import functools

import jax
import jax.numpy as jnp
from jax import lax
from jax.experimental import pallas as pl
from jax.experimental.pallas import tpu as pltpu

F32 = jnp.float32
BF16 = jnp.bfloat16

D_MODEL = 1024
HEAD_DIM = 64
BLOCK = 128
EPS = 1e-6
NEG_INF = -1e30
SWA_Q_HEADS = 8
SWA_KV_HEADS = 2
SWA_WINDOW = 128
DIL_PATTERNS = ((128, 1), (512, 4), (2048, 16))
N_DIL = 3
DIL_HEADS = 4
SSM_GROUP = 16
SSM_GROUPS = 32
SSM_WIDTH = SSM_GROUP * SSM_GROUPS
SSM_STATE = 64
N_STATE = SSM_GROUPS * SSM_STATE
N_BRANCH = 3
FFN_DIM = 2816
CONV_WIDTH = 3

A_Q = SWA_Q_HEADS * HEAD_DIM
A_KV = SWA_KV_HEADS * HEAD_DIM
B_Q = N_DIL * DIL_HEADS * HEAD_DIM
B_KV = DIL_HEADS * HEAD_DIM
GATE_W = N_BRANCH * D_MODEL
W_MAIN = A_Q + 2 * A_KV + B_Q + 2 * B_KV + SSM_WIDTH

IN_QA, IN_KA, IN_VA = 0, A_Q, A_Q + A_KV
IN_QD = A_Q + 2 * A_KV
IN_KD = IN_QD + B_Q
IN_VD = IN_KD + B_KV
IN_U = IN_VD + B_KV

ROW_W = SSM_WIDTH + B_KV + A_KV
ROW_U, ROW_KD, ROW_KA = 0, SSM_WIDTH, SSM_WIDTH + B_KV
COL_W = A_Q + B_Q + B_KV + A_KV
COL_QA, COL_QD, COL_VD, COL_VA = 0, A_Q, A_Q + B_Q, A_Q + B_Q + B_KV
COL_Q_ROWS = A_Q + B_Q

SUBLANES_V7X = 8
MIB = 1024 * 1024

TOK_TILE = 512
ATTN_TILE = 512
ATTN_LANES = 512
SSM_T = 64
SSM_CHUNK = 512
FFN_CHUNK = 256


def _cparams(semantics, vmem_mib):
    return pltpu.CompilerParams(dimension_semantics=semantics, vmem_limit_bytes=vmem_mib * MIB)


def _resident(shape):
    nd = len(shape)
    return pl.BlockSpec(shape, lambda *_: (0,) * nd, pipeline_mode=pl.Buffered(1))


def _rmsnorm(x, g):
    return x * lax.rsqrt(jnp.mean(x * x, axis=-1, keepdims=True) + EPS) * g


def _dot(a, b):
    return jnp.dot(a, b, preferred_element_type=F32)


def _dot_nt(a, b):
    return lax.dot_general(a, b, (((1,), (1,)), ((), ())), preferred_element_type=F32)


def _inproj_body(x_ref, g_ref, wr_ref, wc_ref, row_ref, col_ref):
    h = _rmsnorm(x_ref[...], g_ref[...]).astype(BF16)
    row_ref[...] = _dot(h, wr_ref[...]).astype(BF16)
    col = _dot_nt(wc_ref[...], h)
    col_ref[:COL_Q_ROWS, :] = (col[:COL_Q_ROWS] * (HEAD_DIM ** -0.5)).astype(BF16)
    col_ref[COL_Q_ROWS:, :] = col[COL_Q_ROWS:].astype(BF16)


def _inproj(x, gain, w_row, w_col):
    bsz, s, _ = x.shape
    return pl.pallas_call(
        _inproj_body,
        grid=(bsz, s // TOK_TILE),
        in_specs=[pl.BlockSpec((None, TOK_TILE, D_MODEL), lambda b, i: (b, i, 0)),
                  _resident((1, D_MODEL)), _resident(w_row.shape), _resident(w_col.shape)],
        out_specs=[pl.BlockSpec((None, TOK_TILE, ROW_W), lambda b, i: (b, i, 0)),
                   pl.BlockSpec((None, COL_W, TOK_TILE), lambda b, i: (b, 0, i))],
        out_shape=[jax.ShapeDtypeStruct((bsz, s, ROW_W), BF16),
                   jax.ShapeDtypeStruct((bsz, COL_W, s), BF16)],
        compiler_params=_cparams(("parallel", "parallel"), 40),
        name="inproj",
    )(x, gain, w_row, w_col)


def _attn_body(*refs, n_kv, rep, max_off, tq, use_sink, want_lse):
    qt_ref, kp_ref, k_ref, vtp_ref, vt_ref = refs[:5]
    pos = 5
    sink_ref = None
    if use_sink:
        sink_ref = refs[pos]
        pos += 1
    o_ref = refs[pos]
    lse_ref = refs[pos + 1] if want_lse else None

    i = pl.program_id(1)
    kpos = lax.broadcasted_iota(jnp.int32, (2 * BLOCK, BLOCK), 0)
    qidx = lax.broadcasted_iota(jnp.int32, (2 * BLOCK, BLOCK), 1)
    hi = qidx + BLOCK
    lo = hi - max_off
    band_cap = jnp.where(kpos <= hi, jnp.where(kpos >= lo, jnp.inf, NEG_INF), NEG_INF).astype(F32)
    first_lo = jnp.where(i > 0, 0, BLOCK)
    first_cap = jnp.where(kpos >= first_lo, band_cap, NEG_INF)
    zero_q = jnp.zeros((HEAD_DIM, rep * BLOCK), BF16)
    kv_unit = max(1, ATTN_LANES // (rep * BLOCK))
    n_heads = n_kv * rep

    for j in range(tq // BLOCK):
        cap = first_cap if j == 0 else band_cap
        cap = jnp.concatenate([cap] * (kv_unit * rep), axis=1)
        cur = slice(j * BLOCK, (j + 1) * BLOCK)
        if j == 0:
            k_prev, vt_prev = kp_ref[...], vtp_ref[...]
        else:
            prv = slice((j - 1) * BLOCK, j * BLOCK)
            k_prev, vt_prev = k_ref[prv, :], vt_ref[:, prv]
        kk = jnp.concatenate([k_prev, k_ref[cur, :]], axis=0)
        vvt = jnp.concatenate([vt_prev, vt_ref[:, cur]], axis=1)
        out_t, lse_t = [], []
        for g0 in range(0, n_kv, kv_unit):
            unit = range(g0, g0 + kv_unit)
            heads = [g * rep + r for g in unit for r in range(rep)]
            qbd = jnp.concatenate(
                [jnp.concatenate(
                    [jnp.concatenate([qt_ref[(g * rep + r) * HEAD_DIM:(g * rep + r + 1) * HEAD_DIM, cur]
                                      for r in range(rep)], axis=1) if g2 == g else zero_q
                     for g2 in range(n_kv)], axis=0)
                 for g in unit], axis=1)
            st = jnp.minimum(_dot(kk, qbd), cap)
            m = jnp.max(st, axis=0, keepdims=True)
            if use_sink:
                sk = jnp.concatenate([jnp.full((1, BLOCK), sink_ref[h], F32) for h in heads], axis=1)
                m = jnp.maximum(m, sk)
            p = jnp.exp(st - m)
            l = jnp.sum(p, axis=0, keepdims=True)
            if use_sink:
                l = l + jnp.exp(sk - m)
            ot_all = _dot(vvt[g0 * HEAD_DIM:(g0 + kv_unit) * HEAD_DIM, :], p.astype(BF16)) * (1.0 / l)
            lse = m + jnp.log(l) if want_lse else None
            for n, h in enumerate(heads):
                g = h // rep
                own = ot_all[(g - g0) * HEAD_DIM:(g - g0 + 1) * HEAD_DIM, n * BLOCK:(n + 1) * BLOCK]
                out_t.append(own)
                if want_lse:
                    lse_t.append(jnp.broadcast_to(lse[:, n * BLOCK:(n + 1) * BLOCK], (HEAD_DIM, BLOCK)))
        assert len(out_t) == n_heads
        o_ref[cur, :] = jnp.concatenate(out_t, axis=0).T.astype(o_ref.dtype)
        if want_lse:
            lse_ref[cur, :] = jnp.concatenate(lse_t, axis=0).T


def _banded_attention(qt_arr, q_blk, k_arr, k_blk, vt_arr, v_blk, *, n_kv, rep, max_off, sinks=None,
                      want_lse=False):
    n, seq_len, _ = k_arr.shape
    tq = min(ATTN_TILE, seq_len)
    per = tq // BLOCK
    qw = n_kv * rep * HEAD_DIM
    kw = n_kv * HEAD_DIM
    prev_blk = lambda i: jnp.maximum(i * per - 1, 0)
    in_specs = [pl.BlockSpec((None, qw, tq), lambda b, i: (b, q_blk, i)),
                pl.BlockSpec((None, BLOCK, kw), lambda b, i: (b, prev_blk(i), k_blk)),
                pl.BlockSpec((None, tq, kw), lambda b, i: (b, i, k_blk)),
                pl.BlockSpec((None, kw, BLOCK), lambda b, i: (b, v_blk, prev_blk(i))),
                pl.BlockSpec((None, kw, tq), lambda b, i: (b, v_blk, i))]
    args = [qt_arr, k_arr, k_arr, vt_arr, vt_arr]
    if sinks is not None:
        in_specs.append(pl.BlockSpec(memory_space=pltpu.SMEM))
        args.append(sinks)
    out_spec = pl.BlockSpec((None, tq, qw), lambda b, i: (b, i, 0))
    out_shape = [jax.ShapeDtypeStruct((n, seq_len, qw), BF16)]
    out_specs = [out_spec]
    if want_lse:
        out_shape.append(jax.ShapeDtypeStruct((n, seq_len, qw), F32))
        out_specs.append(out_spec)
    body = functools.partial(_attn_body, n_kv=n_kv, rep=rep, max_off=max_off, tq=tq,
                             use_sink=sinks is not None, want_lse=want_lse)
    res = pl.pallas_call(
        body,
        grid=(n, seq_len // tq),
        in_specs=in_specs,
        out_specs=out_specs,
        out_shape=out_shape,
        compiler_params=_cparams(("parallel", "parallel"), 32),
        name="banded_attention",
    )(*args)
    return res if want_lse else res[0]


def _ssm_param_body(lr_ref, li_ref, ldt_ref, br_ref, bi_ref, ar_ref, ai_ref, bbr_ref, bbi_ref):
    lr, li = lr_ref[...], li_ref[...]
    dt = jnp.exp(ldt_ref[...])
    mag = jnp.exp(lr * dt)
    ab_re, ab_im = mag * jnp.cos(li * dt), mag * jnp.sin(li * dt)
    nr, ni = ab_re - 1.0, ab_im
    den = lr * lr + li * li
    f_re = (nr * lr + ni * li) / den
    f_im = (ni * lr - nr * li) / den
    br, bi = br_ref[...], bi_ref[...]
    ar_ref[...] = ab_re
    ai_ref[...] = ab_im
    bbr_ref[...] = f_re * br - f_im * bi
    bbi_ref[...] = f_re * bi + f_im * br


def _ssm_params(lam_re, lam_im, log_dt, b_re, b_im):
    nl = lam_re.shape[0]
    rows = nl * SSM_GROUPS * SSM_GROUP
    rep = lambda a: jnp.broadcast_to(a[:, :, None, :], (nl, SSM_GROUPS, SSM_GROUP, SSM_STATE)).reshape(
        rows, SSM_STATE)
    ldt = jnp.broadcast_to(log_dt[:, :, None, None], (nl, SSM_GROUPS, SSM_GROUP, SSM_STATE)).reshape(
        rows, SSM_STATE)
    bt = lambda b: jnp.swapaxes(b, 2, 3).reshape(rows, SSM_STATE)
    shape = jax.ShapeDtypeStruct((rows, SSM_STATE), F32)
    ar, ai, bbr, bbi = pl.pallas_call(
        _ssm_param_body, out_shape=[shape] * 4, name="ssm_params",
    )(rep(lam_re), rep(lam_im), ldt, bt(b_re), bt(b_im))
    pick = lambda a: a.reshape(nl, SSM_GROUPS, SSM_GROUP, SSM_STATE)[:, :, 0, :].reshape(nl, N_STATE)
    unflat = lambda a: a.reshape(nl, SSM_GROUPS, SSM_GROUP, SSM_STATE)
    return pick(ar), pick(ai), unflat(bbr), unflat(bbi)


SSM_HALF_GROUPS = SSM_GROUPS // 2
SSM_HALF_IN = SSM_HALF_GROUPS * SSM_GROUP
SSM_HALF_STATE = SSM_HALF_GROUPS * SSM_STATE


def _ssm_matrices(bb_re, bb_im, c_re, c_im):
    eye = jnp.eye(SSM_HALF_GROUPS, dtype=F32)

    def in_blocks(bb):
        bb = bb.reshape(2, SSM_HALF_GROUPS, SSM_GROUP, SSM_STATE)
        return jnp.einsum("ab,kahp->kahbp", eye, bb).reshape(2, SSM_HALF_IN, SSM_HALF_STATE)

    def out_blocks(c):
        c = c.reshape(2, SSM_HALF_GROUPS, SSM_GROUP, SSM_STATE)
        return jnp.einsum("ab,kahp->kapbh", eye, c).reshape(2, SSM_HALF_STATE, SSM_HALF_IN)

    b_mat = jnp.concatenate([in_blocks(bb_re), in_blocks(bb_im)], axis=2).astype(BF16)
    c_mat = jnp.concatenate([out_blocks(c_re), out_blocks(-c_im)], axis=1).astype(BF16)
    return b_mat, c_mat


def _ssm_body(u_ref, bm_ref, ar_ref, ai_ref, cm_ref, d_ref, wg_ref, bg_ref, o_ref,
              x_scr, sr_scr, si_scr):
    nb = SUBLANES_V7X
    width = 2 * SSM_HALF_STATE

    @pl.when(pl.program_id(0) == 0)
    def _():
        sr_scr[...] = jnp.zeros_like(sr_scr)
        si_scr[...] = jnp.zeros_like(si_scr)

    u = u_ref[...]
    for half in range(2):
        ucols = slice(half * SSM_HALF_IN, (half + 1) * SSM_HALF_IN)
        x_scr[:, half * width:(half + 1) * width] = _dot(u[:, ucols], bm_ref[half])

    for half in range(2):
        for c in range(SSM_HALF_STATE // SSM_CHUNK):
            re0 = half * width + c * SSM_CHUNK
            im0 = re0 + SSM_HALF_STATE
            st0 = half * SSM_HALF_STATE + c * SSM_CHUNK
            a_re = ar_ref[:, st0:st0 + SSM_CHUNK]
            a_im = ai_ref[:, st0:st0 + SSM_CHUNK]

            def step(t, carry, re0=re0, im0=im0, a_re=a_re, a_im=a_im):
                xr, xi = carry
                rows = pl.ds(pl.multiple_of(t * nb, nb), nb)
                nr = a_re * xr - a_im * xi + x_scr[rows, re0:re0 + SSM_CHUNK]
                ni = a_re * xi + a_im * xr + x_scr[rows, im0:im0 + SSM_CHUNK]
                x_scr[rows, re0:re0 + SSM_CHUNK] = nr
                x_scr[rows, im0:im0 + SSM_CHUNK] = ni
                return nr, ni

            xr, xi = lax.fori_loop(0, SSM_T, step,
                                   (sr_scr[:, st0:st0 + SSM_CHUNK], si_scr[:, st0:st0 + SSM_CHUNK]),
                                   unroll=8)
            sr_scr[:, st0:st0 + SSM_CHUNK] = xr
            si_scr[:, st0:st0 + SSM_CHUNK] = xi

    ys = [_dot(x_scr[:, half * width:(half + 1) * width].astype(BF16), cm_ref[half]) for half in range(2)]
    y = jnp.concatenate(ys, axis=1) + d_ref[...] * u.astype(F32)
    z = jax.nn.gelu(y)
    z = z * jax.nn.sigmoid(_dot(z.astype(BF16), wg_ref[...]) + bg_ref[...])
    o_ref[...] = z.astype(o_ref.dtype)


def _ssm(u_tm, b_mat, a_re, a_im, c_mat, d_skip, w_glu, b_glu):
    rows_total = u_tm.shape[0]
    rows = SSM_T * SUBLANES_V7X
    return pl.pallas_call(
        _ssm_body,
        grid=(rows_total // rows,),
        in_specs=[pl.BlockSpec((rows, SSM_WIDTH), lambda i: (i, 0)),
                  _resident(b_mat.shape), _resident(a_re.shape), _resident(a_im.shape),
                  _resident(c_mat.shape), _resident(d_skip.shape), _resident(w_glu.shape),
                  _resident(b_glu.shape)],
        out_specs=pl.BlockSpec((rows, SSM_WIDTH), lambda i: (i, 0)),
        out_shape=jax.ShapeDtypeStruct((rows_total, SSM_WIDTH), BF16),
        scratch_shapes=[pltpu.VMEM((rows, 2 * N_STATE), F32),
                        pltpu.VMEM((SUBLANES_V7X, N_STATE), F32),
                        pltpu.VMEM((SUBLANES_V7X, N_STATE), F32)],
        compiler_params=_cparams(("arbitrary",), 40),
        name="s5_mixer",
    )(u_tm, b_mat, a_re, a_im, c_mat, d_skip, w_glu, b_glu)


def _merge_body(x_ref, ya_ref, o0_ref, o1_ref, o2_ref, l0_ref, l1_ref, l2_ref, yc_ref,
                gn_ref, wg_ref, wa_ref, wb_ref, wc_ref, wo_ref, out_ref, merged_scr):
    x = x_ref[...]
    h = _rmsnorm(x, gn_ref[...]).astype(BF16)
    l0, l1, l2 = l0_ref[...], l1_ref[...], l2_ref[...]
    top = jnp.maximum(jnp.maximum(l0, l1), l2)
    e0, e1, e2 = jnp.exp(l0 - top), jnp.exp(l1 - top), jnp.exp(l2 - top)
    inv = 1.0 / (e0 + e1 + e2)
    yb = ((e0 * inv) * o0_ref[...].astype(F32) + (e1 * inv) * o1_ref[...].astype(F32)
          + (e2 * inv) * o2_ref[...].astype(F32)).astype(BF16)
    ya = ya_ref[...]
    yc = yc_ref[...]
    step = 256
    for c in range(D_MODEL // step):
        cs = slice(c * step, (c + 1) * step)
        gate = lambda k: jax.nn.sigmoid(_dot(h, wg_ref[:, k * D_MODEL + c * step:k * D_MODEL + (c + 1) * step]))
        merged = (gate(0) * _dot(ya, wa_ref[:, cs]) + gate(1) * _dot(yb, wb_ref[:, cs])
                  + gate(2) * _dot(yc, wc_ref[:, cs]))
        merged_scr[:, cs] = merged.astype(BF16)
    out_ref[...] = x + _dot(merged_scr[...], wo_ref[...])


def _merge(x, ya, outs, lses, yc, gain, w_gate, w_a, w_b, w_c, w_out):
    bsz, s, _ = x.shape
    tile = lambda width: pl.BlockSpec((None, TOK_TILE, width), lambda b, i: (b, i, 0))
    return pl.pallas_call(
        _merge_body,
        grid=(bsz, s // TOK_TILE),
        in_specs=[tile(D_MODEL), tile(A_Q)] + [tile(B_KV)] * 6 + [tile(SSM_WIDTH),
                  _resident(gain.shape), _resident(w_gate.shape), _resident(w_a.shape),
                  _resident(w_b.shape), _resident(w_c.shape), _resident(w_out.shape)],
        out_specs=tile(D_MODEL),
        out_shape=jax.ShapeDtypeStruct(x.shape, x.dtype),
        scratch_shapes=[pltpu.VMEM((TOK_TILE, D_MODEL), BF16)],
        compiler_params=_cparams(("parallel", "parallel"), 48),
        name="gated_merge",
    )(x, ya, *outs, *lses, yc, gain, w_gate, w_a, w_b, w_c, w_out)


def _ffn_body(*refs, final):
    x_ref, gn_ref, wup_ref, cw_ref, cb_ref, wdn_ref = refs[:6]
    pos = 6
    gf_ref = None
    if final:
        gf_ref = refs[pos]
        pos += 1
    out_ref, carry_scr, work_scr, act_scr = refs[pos:pos + 4]
    pad = SUBLANES_V7X

    @pl.when(pl.program_id(1) == 0)
    def _():
        carry_scr[...] = jnp.zeros_like(carry_scr)

    x = x_ref[...]
    h = _rmsnorm(x, gn_ref[...]).astype(BF16)

    def conv_cols(slot, c0):
        cs = slice(c0, c0 + FFN_CHUNK)
        up = _dot(h, wup_ref[:, cs])
        work_scr[slot, 0:pad, :] = carry_scr[:, cs]
        work_scr[slot, pad:pad + TOK_TILE, :] = up
        carry_scr[:, cs] = up[TOK_TILE - pad:, :]
        acc = cb_ref[:, cs] + cw_ref[CONV_WIDTH - 1:CONV_WIDTH, cs] * up
        for k in range(CONV_WIDTH - 1):
            shift = CONV_WIDTH - 1 - k
            acc = acc + cw_ref[k:k + 1, cs] * work_scr[slot, pad - shift:pad - shift + TOK_TILE, :]
        return acc

    for c in range(FFN_DIM // FFN_CHUNK):
        gate = conv_cols(0, c * FFN_CHUNK)
        val = conv_cols(1, FFN_DIM + c * FFN_CHUNK)
        act_scr[:, c * FFN_CHUNK:(c + 1) * FFN_CHUNK] = (jax.nn.silu(gate) * val).astype(BF16)
    y = x + _dot(act_scr[...], wdn_ref[...])
    if final:
        y = _rmsnorm(y, gf_ref[...])
    out_ref[...] = y


def _ffn(x, gain, w_up, conv_w, conv_b, w_down, final_gain=None):
    bsz, s, _ = x.shape
    final = final_gain is not None
    tile = pl.BlockSpec((None, TOK_TILE, D_MODEL), lambda b, i: (b, i, 0))
    in_specs = [tile, _resident(gain.shape), _resident(w_up.shape), _resident(conv_w.shape),
                _resident(conv_b.shape), _resident(w_down.shape)]
    args = [x, gain, w_up, conv_w, conv_b, w_down]
    if final:
        in_specs.append(_resident(final_gain.shape))
        args.append(final_gain)
    return pl.pallas_call(
        functools.partial(_ffn_body, final=final),
        grid=(bsz, s // TOK_TILE),
        in_specs=in_specs,
        out_specs=tile,
        out_shape=jax.ShapeDtypeStruct(x.shape, x.dtype),
        scratch_shapes=[pltpu.VMEM((SUBLANES_V7X, 2 * FFN_DIM), F32),
                        pltpu.VMEM((2, SUBLANES_V7X + TOK_TILE, FFN_CHUNK), F32),
                        pltpu.VMEM((TOK_TILE, FFN_DIM), BF16)],
        compiler_params=_cparams(("arbitrary", "arbitrary"), 56),
        name="conv_ffn",
    )(*args)


def _to_sub(a, dil):
    b, s, c = a.shape
    return a.reshape(b, s // dil, dil, c).transpose(0, 2, 1, 3).reshape(b * dil, s // dil, c)


def _to_sub_t(a, dil):
    b, c, s = a.shape
    return a.reshape(b, c, s // dil, dil).transpose(0, 3, 1, 2).reshape(b * dil, c, s // dil)


def _from_sub(a, b, dil):
    n, length, c = a.shape
    return a.reshape(b, dil, length, c).transpose(0, 2, 1, 3).reshape(b, length * dil, c)


def kernel(x, norm_mix, w_in, attn_sinks, ssm_lambda_re, ssm_lambda_im, ssm_log_dt, ssm_b_re, ssm_b_im,
           ssm_c_re, ssm_c_im, ssm_d, w_glu, b_glu, w_branch_a, w_branch_b, w_branch_c, w_out,
           norm_ffn, w_up, conv_w, conv_b, w_down, norm_final):
    bsz, s, _ = x.shape
    depth = w_in.shape[0]
    assert bsz == SUBLANES_V7X, "the S5 kernel maps the batch onto the sublane axis"
    assert s % TOK_TILE == 0 and s % SSM_T == 0
    assert all((s // dil) % BLOCK == 0 for _, dil in DIL_PATTERNS)

    a_re, a_im, bb_re, bb_im = _ssm_params(ssm_lambda_re, ssm_lambda_im, ssm_log_dt, ssm_b_re, ssm_b_im)
    row = lambda v: v.reshape(1, -1)
    kv_w = B_KV

    for l in range(depth):
        wl = w_in[l]
        w_row = jnp.concatenate([wl[:, IN_U:IN_U + SSM_WIDTH], wl[:, IN_KD:IN_KD + B_KV],
                                 wl[:, IN_KA:IN_KA + A_KV]], axis=1).astype(BF16)
        w_col = jnp.concatenate([wl[:, IN_QA:IN_QA + A_Q], wl[:, IN_QD:IN_QD + B_Q],
                                 wl[:, IN_VD:IN_VD + B_KV], wl[:, IN_VA:IN_VA + A_KV]], axis=1).T.astype(BF16)
        w_gate = wl[:, W_MAIN:].astype(BF16)
        rows, cols = _inproj(x, row(norm_mix[l]), w_row, w_col)

        ya = _banded_attention(cols, COL_QA // A_Q, rows, ROW_KA // A_KV, cols, COL_VA // A_KV,
                               n_kv=SWA_KV_HEADS, rep=SWA_Q_HEADS // SWA_KV_HEADS,
                               max_off=SWA_WINDOW - 1, sinks=attn_sinks[l])

        outs, lses = [], []
        for gi, (window, dil) in enumerate(DIL_PATTERNS):
            q0 = COL_QD + gi * kv_w
            if dil == 1:
                o, lse = _banded_attention(cols, q0 // kv_w, rows, ROW_KD // kv_w, cols, COL_VD // kv_w,
                                           n_kv=DIL_HEADS, rep=1, max_off=window // dil, want_lse=True)
            else:
                sub_k = _to_sub(rows[:, :, ROW_KD:ROW_KD + kv_w], dil)
                sub_qv = _to_sub_t(jnp.concatenate([cols[:, q0:q0 + kv_w, :],
                                                    cols[:, COL_VD:COL_VD + kv_w, :]], axis=1), dil)
                o, lse = _banded_attention(sub_qv, 0, sub_k, 0, sub_qv, 1, n_kv=DIL_HEADS, rep=1,
                                           max_off=window // dil, want_lse=True)
                o, lse = _from_sub(o, bsz, dil), _from_sub(lse, bsz, dil)
            outs.append(o)
            lses.append(lse)

        u_tm = rows[:, :, ROW_U:ROW_U + SSM_WIDTH].transpose(1, 0, 2).reshape(s * bsz, SSM_WIDTH)
        b_mat, c_mat = _ssm_matrices(bb_re[l], bb_im[l], ssm_c_re[l], ssm_c_im[l])
        bcast = lambda v: jnp.broadcast_to(v[None, :], (SUBLANES_V7X, N_STATE))
        yc_tm = _ssm(u_tm, b_mat, bcast(a_re[l]), bcast(a_im[l]), c_mat, row(ssm_d[l]),
                     w_glu[l].astype(BF16), row(b_glu[l]))
        yc = yc_tm.reshape(s, bsz, SSM_WIDTH).transpose(1, 0, 2)

        x = _merge(x, ya, outs, lses, yc, row(norm_mix[l]), w_gate, w_branch_a[l].astype(BF16),
                   w_branch_b[l].astype(BF16), w_branch_c[l].astype(BF16), w_out[l].astype(BF16))
        x = _ffn(x, row(norm_ffn[l]), w_up[l].astype(BF16), conv_w[l], row(conv_b[l]),
                 w_down[l].astype(BF16), final_gain=row(norm_final) if l == depth - 1 else None)
    return x
```

```python
import functools

import jax
import jax.numpy as jnp
from jax import lax
from jax.experimental import pallas as pl
from jax.experimental.pallas import tpu as pltpu

F32 = jnp.float32
BF16 = jnp.bfloat16

D_MODEL = 1024
HEAD_DIM = 64
BLOCK = 128
EPS = 1e-6
NEG_INF = -1e30
SWA_Q_HEADS = 8
SWA_KV_HEADS = 2
SWA_WINDOW = 128
DIL_PATTERNS = ((128, 1), (512, 4), (2048, 16))
N_DIL = 3
DIL_HEADS = 4
SSM_GROUP = 16
SSM_GROUPS = 32
SSM_WIDTH = SSM_GROUP * SSM_GROUPS
SSM_STATE = 64
N_STATE = SSM_GROUPS * SSM_STATE
N_BRANCH = 3
FFN_DIM = 2816
CONV_WIDTH = 3

A_Q = SWA_Q_HEADS * HEAD_DIM
A_KV = SWA_KV_HEADS * HEAD_DIM
B_Q = N_DIL * DIL_HEADS * HEAD_DIM
B_KV = DIL_HEADS * HEAD_DIM
GATE_W = N_BRANCH * D_MODEL
W_MAIN = A_Q + 2 * A_KV + B_Q + 2 * B_KV + SSM_WIDTH

IN_QA, IN_KA, IN_VA = 0, A_Q, A_Q + A_KV
IN_QD = A_Q + 2 * A_KV
IN_KD = IN_QD + B_Q
IN_VD = IN_KD + B_KV
IN_U = IN_VD + B_KV

ROW_W = SSM_WIDTH + B_KV + A_KV
ROW_U, ROW_KD, ROW_KA = 0, SSM_WIDTH, SSM_WIDTH + B_KV
COL_W = A_Q + B_Q + B_KV + A_KV
COL_QA, COL_QD, COL_VD, COL_VA = 0, A_Q, A_Q + B_Q, A_Q + B_Q + B_KV
COL_Q_ROWS = A_Q + B_Q

SUBLANES_V7X = 8
LANES_V7X = 128
MIB = 1024 * 1024

TOK_TILE = 512
ATTN_TILE = 1024
ATTN_LANES = 1024
SSM_T = 64
SSM_CHUNK = 512
FFN_CHUNK = 256


def _cparams(semantics, vmem_mib):
    return pltpu.CompilerParams(dimension_semantics=semantics, vmem_limit_bytes=vmem_mib * MIB)


def _resident(shape):
    nd = len(shape)
    return pl.BlockSpec(shape, lambda *_: (0,) * nd, pipeline_mode=pl.Buffered(1))


def _rmsnorm(x, g):
    return x * lax.rsqrt(jnp.mean(x * x, axis=-1, keepdims=True) + EPS) * g


def _dot(a, b):
    return jnp.dot(a, b, preferred_element_type=F32)


def _dot_nt(a, b):
    return lax.dot_general(a, b, (((1,), (1,)), ((), ())), preferred_element_type=F32)


def _inproj_body(x_ref, g_ref, wr_ref, wc_ref, row_ref, col_ref):
    h = _rmsnorm(x_ref[...], g_ref[...]).astype(BF16)
    row_ref[...] = _dot(h, wr_ref[...]).astype(BF16)
    col = _dot_nt(wc_ref[...], h)
    col_ref[:COL_Q_ROWS, :] = (col[:COL_Q_ROWS] * (HEAD_DIM ** -0.5)).astype(BF16)
    col_ref[COL_Q_ROWS:, :] = col[COL_Q_ROWS:].astype(BF16)


def _inproj(x, gain, w_row, w_col):
    bsz, s, _ = x.shape
    return pl.pallas_call(
        _inproj_body,
        grid=(bsz, s // TOK_TILE),
        in_specs=[pl.BlockSpec((None, TOK_TILE, D_MODEL), lambda b, i: (b, i, 0)),
                  _resident((1, D_MODEL)), _resident(w_row.shape), _resident(w_col.shape)],
        out_specs=[pl.BlockSpec((None, TOK_TILE, ROW_W), lambda b, i: (b, i, 0)),
                   pl.BlockSpec((None, COL_W, TOK_TILE), lambda b, i: (b, 0, i))],
        out_shape=[jax.ShapeDtypeStruct((bsz, s, ROW_W), BF16),
                   jax.ShapeDtypeStruct((bsz, COL_W, s), BF16)],
        compiler_params=_cparams(("parallel", "parallel"), 40),
        name="inproj",
    )(x, gain, w_row, w_col)


def _attn_body(*refs, n_kv, rep, max_off, tq, nseq, use_sink, want_lse):
    qt_ref, kp_ref, k_ref, vtp_ref, vt_ref = refs[:5]
    pos = 5
    sink_ref = None
    if use_sink:
        sink_ref = refs[pos]
        pos += 1
    o_ref = refs[pos]
    lse_ref = refs[pos + 1] if want_lse else None

    i = pl.program_id(1)
    kpos = lax.broadcasted_iota(jnp.int32, (2 * BLOCK, BLOCK), 0)
    qidx = lax.broadcasted_iota(jnp.int32, (2 * BLOCK, BLOCK), 1)
    hi = qidx + BLOCK
    lo = hi - max_off
    band_cap = jnp.where(kpos <= hi, jnp.where(kpos >= lo, jnp.inf, NEG_INF), NEG_INF).astype(F32)
    first_lo = jnp.where(i > 0, 0, BLOCK)
    first_cap = jnp.where(kpos >= first_lo, band_cap, NEG_INF)
    zero_q = jnp.zeros((HEAD_DIM, rep * BLOCK), BF16)
    kv_unit = min(n_kv, max(1, ATTN_LANES // (rep * BLOCK)))
    n_heads = n_kv * rep

    blocks = [(sq, j) for sq in range(nseq) for j in range(tq // BLOCK)]
    units = []
    for sq, j in blocks:
        cap = first_cap if j == 0 else band_cap
        cap = jnp.concatenate([cap] * (kv_unit * rep), axis=1)
        cur = slice(j * BLOCK, (j + 1) * BLOCK)
        if j == 0:
            k_prev, vt_prev = kp_ref[sq], vtp_ref[sq]
        else:
            prv = slice((j - 1) * BLOCK, j * BLOCK)
            k_prev, vt_prev = k_ref[sq, prv, :], vt_ref[sq, :, prv]
        kk = jnp.concatenate([k_prev, k_ref[sq, cur, :]], axis=0)
        vvt = jnp.concatenate([vt_prev, vt_ref[sq, :, cur]], axis=1)
        for g0 in range(0, n_kv, kv_unit):
            unit = range(g0, g0 + kv_unit)
            heads = [g * rep + r for g in unit for r in range(rep)]
            qbd = jnp.concatenate(
                [jnp.concatenate(
                    [jnp.concatenate([qt_ref[sq, (g * rep + r) * HEAD_DIM:(g * rep + r + 1) * HEAD_DIM, cur]
                                      for r in range(rep)], axis=1) if g2 == g else zero_q
                     for g2 in range(n_kv)], axis=0)
                 for g in unit], axis=1)
            st = jnp.minimum(_dot(kk, qbd), cap)
            units.append(((sq, j), g0, heads, st, vvt[g0 * HEAD_DIM:(g0 + kv_unit) * HEAD_DIM, :]))

    soft = []
    for _, g0, heads, st, vt_u in units:
        m = jnp.max(st, axis=0, keepdims=True)
        if use_sink:
            sk = jnp.concatenate([jnp.full((1, BLOCK), sink_ref[h], F32) for h in heads], axis=1)
            m = jnp.maximum(m, sk)
        p = jnp.exp(st - m)
        l = jnp.sum(p, axis=0, keepdims=True)
        if use_sink:
            l = l + jnp.exp(sk - m)
        soft.append((p.astype(BF16), 1.0 / l, m + jnp.log(l) if want_lse else None))

    out_t = {blk: [] for blk in blocks}
    lse_t = {blk: [] for blk in blocks}
    for (blk, g0, heads, st, vt_u), (p, inv_l, lse) in zip(units, soft):
        ot_all = _dot(vt_u, p) * inv_l
        for n, h in enumerate(heads):
            g = h // rep
            out_t[blk].append(ot_all[(g - g0) * HEAD_DIM:(g - g0 + 1) * HEAD_DIM, n * BLOCK:(n + 1) * BLOCK])
            if want_lse:
                lse_t[blk].append(jnp.broadcast_to(lse[:, n * BLOCK:(n + 1) * BLOCK], (HEAD_DIM, BLOCK)))
    for sq, j in blocks:
        cur = slice(j * BLOCK, (j + 1) * BLOCK)
        assert len(out_t[sq, j]) == n_heads
        o_ref[sq, cur, :] = jnp.concatenate(out_t[sq, j], axis=0).T.astype(o_ref.dtype)
        if want_lse:
            lse_ref[sq, cur, :] = jnp.concatenate(lse_t[sq, j], axis=0).T


def _banded_attention(qt_arr, q_blk, k_arr, k_blk, vt_arr, v_blk, *, n_kv, rep, max_off, sinks=None,
                      want_lse=False):
    n, seq_len, _ = k_arr.shape
    tq = min(ATTN_TILE, seq_len)
    nseq = ATTN_TILE // tq
    assert seq_len % tq == 0 and n % nseq == 0
    per = tq // BLOCK
    qw = n_kv * rep * HEAD_DIM
    kw = n_kv * HEAD_DIM
    prev_blk = lambda i: jnp.maximum(i * per - 1, 0)
    in_specs = [pl.BlockSpec((nseq, qw, tq), lambda b, i: (b, q_blk, i)),
                pl.BlockSpec((nseq, BLOCK, kw), lambda b, i: (b, prev_blk(i), k_blk)),
                pl.BlockSpec((nseq, tq, kw), lambda b, i: (b, i, k_blk)),
                pl.BlockSpec((nseq, kw, BLOCK), lambda b, i: (b, v_blk, prev_blk(i))),
                pl.BlockSpec((nseq, kw, tq), lambda b, i: (b, v_blk, i))]
    args = [qt_arr, k_arr, k_arr, vt_arr, vt_arr]
    if sinks is not None:
        in_specs.append(pl.BlockSpec(memory_space=pltpu.SMEM))
        args.append(sinks)
    out_spec = pl.BlockSpec((nseq, tq, qw), lambda b, i: (b, i, 0))
    out_shape = [jax.ShapeDtypeStruct((n, seq_len, qw), BF16)]
    out_specs = [out_spec]
    if want_lse:
        out_shape.append(jax.ShapeDtypeStruct((n, seq_len, qw), F32))
        out_specs.append(out_spec)
    body = functools.partial(_attn_body, n_kv=n_kv, rep=rep, max_off=max_off, tq=tq, nseq=nseq,
                             use_sink=sinks is not None, want_lse=want_lse)
    res = pl.pallas_call(
        body,
        grid=(n // nseq, seq_len // tq),
        in_specs=in_specs,
        out_specs=out_specs,
        out_shape=out_shape,
        compiler_params=_cparams(("parallel", "parallel"), 48),
        name="banded_attention",
    )(*args)
    return res if want_lse else res[0]


def _ssm_param_body(lr_ref, li_ref, ldt_ref, br_ref, bi_ref, ar_ref, ai_ref, bbr_ref, bbi_ref):
    lr, li = lr_ref[...], li_ref[...]
    dt = jnp.exp(ldt_ref[...])
    mag = jnp.exp(lr * dt)
    ab_re, ab_im = mag * jnp.cos(li * dt), mag * jnp.sin(li * dt)
    nr, ni = ab_re - 1.0, ab_im
    den = lr * lr + li * li
    f_re = (nr * lr + ni * li) / den
    f_im = (ni * lr - nr * li) / den
    br, bi = br_ref[...], bi_ref[...]
    ar_ref[...] = ab_re
    ai_ref[...] = ab_im
    bbr_ref[...] = f_re * br - f_im * bi
    bbi_ref[...] = f_re * bi + f_im * br


def _ssm_params(lam_re, lam_im, log_dt, b_re, b_im):
    nl = lam_re.shape[0]
    rows = nl * SSM_GROUPS * SSM_GROUP
    rep = lambda a: jnp.broadcast_to(a[:, :, None, :], (nl, SSM_GROUPS, SSM_GROUP, SSM_STATE)).reshape(
        rows, SSM_STATE)
    ldt = jnp.broadcast_to(log_dt[:, :, None, None], (nl, SSM_GROUPS, SSM_GROUP, SSM_STATE)).reshape(
        rows, SSM_STATE)
    bt = lambda b: jnp.swapaxes(b, 2, 3).reshape(rows, SSM_STATE)
    shape = jax.ShapeDtypeStruct((rows, SSM_STATE), F32)
    ar, ai, bbr, bbi = pl.pallas_call(
        _ssm_param_body, out_shape=[shape] * 4, name="ssm_params",
    )(rep(lam_re), rep(lam_im), ldt, bt(b_re), bt(b_im))
    pick = lambda a: a.reshape(nl, SSM_GROUPS, SSM_GROUP, SSM_STATE)[:, :, 0, :].reshape(nl, N_STATE)
    unflat = lambda a: a.reshape(nl, SSM_GROUPS, SSM_GROUP, SSM_STATE)
    return pick(ar), pick(ai), unflat(bbr), unflat(bbi)


SSM_HALF_GROUPS = SSM_GROUPS // 2
SSM_HALF_IN = SSM_HALF_GROUPS * SSM_GROUP
SSM_HALF_STATE = SSM_HALF_GROUPS * SSM_STATE


def _ssm_matrices(bb_re, bb_im, c_re, c_im):
    eye = jnp.eye(SSM_HALF_GROUPS, dtype=F32)

    def in_blocks(bb):
        bb = bb.reshape(2, SSM_HALF_GROUPS, SSM_GROUP, SSM_STATE)
        return jnp.einsum("ab,kahp->kahbp", eye, bb).reshape(2, SSM_HALF_IN, SSM_HALF_STATE)

    def out_blocks(c):
        c = c.reshape(2, SSM_HALF_GROUPS, SSM_GROUP, SSM_STATE)
        return jnp.einsum("ab,kahp->kapbh", eye, c).reshape(2, SSM_HALF_STATE, SSM_HALF_IN)

    b_mat = jnp.concatenate([in_blocks(bb_re), in_blocks(bb_im)], axis=2).astype(BF16)
    c_mat = jnp.concatenate([out_blocks(c_re), out_blocks(-c_im)], axis=1).astype(BF16)
    return b_mat, c_mat


def _ssm_body(u_ref, bm_ref, ar_ref, ai_ref, cm_ref, d_ref, wg_ref, bg_ref, o_ref,
              x_scr, sr_scr, si_scr, tb_scr):
    nb = SUBLANES_V7X
    width = 2 * SSM_HALF_STATE
    n_slab = SSM_WIDTH // LANES_V7X

    @pl.when(pl.program_id(0) == 0)
    def _():
        sr_scr[...] = jnp.zeros_like(sr_scr)
        si_scr[...] = jnp.zeros_like(si_scr)

    for b in range(nb):
        ub = u_ref[b].astype(F32)
        for s in range(n_slab):
            tb_scr[s, pl.ds(b, SSM_T, stride=nb), :] = ub[:, s * LANES_V7X:(s + 1) * LANES_V7X]
    u32 = jnp.concatenate([tb_scr[s] for s in range(n_slab)], axis=1)
    u = u32.astype(BF16)
    for half in range(2):
        ucols = slice(half * SSM_HALF_IN, (half + 1) * SSM_HALF_IN)
        x_scr[:, half * width:(half + 1) * width] = _dot(u[:, ucols], bm_ref[half])

    for half in range(2):
        for c in range(SSM_HALF_STATE // SSM_CHUNK):
            re0 = half * width + c * SSM_CHUNK
            im0 = re0 + SSM_HALF_STATE
            st0 = half * SSM_HALF_STATE + c * SSM_CHUNK
            a_re = ar_ref[:, st0:st0 + SSM_CHUNK]
            a_im = ai_ref[:, st0:st0 + SSM_CHUNK]

            def step(t, carry, re0=re0, im0=im0, a_re=a_re, a_im=a_im):
                xr, xi = carry
                rows = pl.ds(pl.multiple_of(t * nb, nb), nb)
                nr = a_re * xr - a_im * xi + x_scr[rows, re0:re0 + SSM_CHUNK]
                ni = a_re * xi + a_im * xr + x_scr[rows, im0:im0 + SSM_CHUNK]
                x_scr[rows, re0:re0 + SSM_CHUNK] = nr
                x_scr[rows, im0:im0 + SSM_CHUNK] = ni
                return nr, ni

            xr, xi = lax.fori_loop(0, SSM_T, step,
                                   (sr_scr[:, st0:st0 + SSM_CHUNK], si_scr[:, st0:st0 + SSM_CHUNK]),
                                   unroll=8)
            sr_scr[:, st0:st0 + SSM_CHUNK] = xr
            si_scr[:, st0:st0 + SSM_CHUNK] = xi

    ys = [_dot(x_scr[:, half * width:(half + 1) * width].astype(BF16), cm_ref[half]) for half in range(2)]
    y = jnp.concatenate(ys, axis=1) + d_ref[...] * u32
    z = jax.nn.gelu(y)
    z = z * jax.nn.sigmoid(_dot(z.astype(BF16), wg_ref[...]) + bg_ref[...])
    for s in range(n_slab):
        tb_scr[s] = z[:, s * LANES_V7X:(s + 1) * LANES_V7X]
    for b in range(nb):
        for s in range(n_slab):
            o_ref[b, :, s * LANES_V7X:(s + 1) * LANES_V7X] = (
                tb_scr[s, pl.ds(b, SSM_T, stride=nb), :].astype(o_ref.dtype))


def _ssm(u_arr, u_blk, b_mat, a_re, a_im, c_mat, d_skip, w_glu, b_glu):
    bsz, s, _ = u_arr.shape
    rows = SSM_T * SUBLANES_V7X
    return pl.pallas_call(
        _ssm_body,
        grid=(s // SSM_T,),
        in_specs=[pl.BlockSpec((bsz, SSM_T, SSM_WIDTH), lambda i: (0, i, u_blk)),
                  _resident(b_mat.shape), _resident(a_re.shape), _resident(a_im.shape),
                  _resident(c_mat.shape), _resident(d_skip.shape), _resident(w_glu.shape),
                  _resident(b_glu.shape)],
        out_specs=pl.BlockSpec((bsz, SSM_T, SSM_WIDTH), lambda i: (0, i, 0)),
        out_shape=jax.ShapeDtypeStruct((bsz, s, SSM_WIDTH), BF16),
        scratch_shapes=[pltpu.VMEM((rows, 2 * N_STATE), F32),
                        pltpu.VMEM((SUBLANES_V7X, N_STATE), F32),
                        pltpu.VMEM((SUBLANES_V7X, N_STATE), F32),
                        pltpu.VMEM((SSM_WIDTH // LANES_V7X, rows, LANES_V7X), F32)],
        compiler_params=_cparams(("arbitrary",), 40),
        name="s5_mixer",
    )(u_arr, b_mat, a_re, a_im, c_mat, d_skip, w_glu, b_glu)


def _merge_body(x_ref, ya_ref, o0_ref, o1_ref, o2_ref, l0_ref, l1_ref, l2_ref, yc_ref,
                gn_ref, wg_ref, wa_ref, wb_ref, wc_ref, wo_ref, out_ref, merged_scr):
    x = x_ref[...]
    h = _rmsnorm(x, gn_ref[...]).astype(BF16)
    l0, l1, l2 = l0_ref[...], l1_ref[...], l2_ref[...]
    top = jnp.maximum(jnp.maximum(l0, l1), l2)
    e0, e1, e2 = jnp.exp(l0 - top), jnp.exp(l1 - top), jnp.exp(l2 - top)
    inv = 1.0 / (e0 + e1 + e2)
    yb = ((e0 * inv) * o0_ref[...].astype(F32) + (e1 * inv) * o1_ref[...].astype(F32)
          + (e2 * inv) * o2_ref[...].astype(F32)).astype(BF16)
    ya = ya_ref[...]
    yc = yc_ref[...]
    step = 256
    for c in range(D_MODEL // step):
        cs = slice(c * step, (c + 1) * step)
        gate = lambda k: jax.nn.sigmoid(_dot(h, wg_ref[:, k * D_MODEL + c * step:k * D_MODEL + (c + 1) * step]))
        merged = (gate(0) * _dot(ya, wa_ref[:, cs]) + gate(1) * _dot(yb, wb_ref[:, cs])
                  + gate(2) * _dot(yc, wc_ref[:, cs]))
        merged_scr[:, cs] = merged.astype(BF16)
    out_ref[...] = x + _dot(merged_scr[...], wo_ref[...])


def _merge(x, ya, outs, lses, yc, gain, w_gate, w_a, w_b, w_c, w_out):
    bsz, s, _ = x.shape
    tile = lambda width: pl.BlockSpec((None, TOK_TILE, width), lambda b, i: (b, i, 0))
    return pl.pallas_call(
        _merge_body,
        grid=(bsz, s // TOK_TILE),
        in_specs=[tile(D_MODEL), tile(A_Q)] + [tile(B_KV)] * 6 + [tile(SSM_WIDTH),
                  _resident(gain.shape), _resident(w_gate.shape), _resident(w_a.shape),
                  _resident(w_b.shape), _resident(w_c.shape), _resident(w_out.shape)],
        out_specs=tile(D_MODEL),
        out_shape=jax.ShapeDtypeStruct(x.shape, x.dtype),
        scratch_shapes=[pltpu.VMEM((TOK_TILE, D_MODEL), BF16)],
        compiler_params=_cparams(("parallel", "parallel"), 48),
        name="gated_merge",
    )(x, ya, *outs, *lses, yc, gain, w_gate, w_a, w_b, w_c, w_out)


def _ffn_body(*refs, final):
    x_ref, gn_ref, wup_ref, cw_ref, cb_ref, wdn_ref = refs[:6]
    pos = 6
    gf_ref = None
    if final:
        gf_ref = refs[pos]
        pos += 1
    out_ref, carry_scr, work_scr, act_scr = refs[pos:pos + 4]
    pad = SUBLANES_V7X

    @pl.when(pl.program_id(1) == 0)
    def _():
        carry_scr[...] = jnp.zeros_like(carry_scr)

    x = x_ref[...]
    h = _rmsnorm(x, gn_ref[...]).astype(BF16)

    def conv_cols(slot, c0):
        cs = slice(c0, c0 + FFN_CHUNK)
        up = _dot(h, wup_ref[:, cs])
        work_scr[slot, 0:pad, :] = carry_scr[:, cs]
        work_scr[slot, pad:pad + TOK_TILE, :] = up
        carry_scr[:, cs] = up[TOK_TILE - pad:, :]
        acc = cb_ref[:, cs] + cw_ref[CONV_WIDTH - 1:CONV_WIDTH, cs] * up
        for k in range(CONV_WIDTH - 1):
            shift = CONV_WIDTH - 1 - k
            acc = acc + cw_ref[k:k + 1, cs] * work_scr[slot, pad - shift:pad - shift + TOK_TILE, :]
        return acc

    for c in range(FFN_DIM // FFN_CHUNK):
        gate = conv_cols(0, c * FFN_CHUNK)
        val = conv_cols(1, FFN_DIM + c * FFN_CHUNK)
        act_scr[:, c * FFN_CHUNK:(c + 1) * FFN_CHUNK] = (jax.nn.silu(gate) * val).astype(BF16)
    y = x + _dot(act_scr[...], wdn_ref[...])
    if final:
        y = _rmsnorm(y, gf_ref[...])
    out_ref[...] = y


def _ffn(x, gain, w_up, conv_w, conv_b, w_down, final_gain=None):
    bsz, s, _ = x.shape
    final = final_gain is not None
    tile = pl.BlockSpec((None, TOK_TILE, D_MODEL), lambda b, i: (b, i, 0))
    in_specs = [tile, _resident(gain.shape), _resident(w_up.shape), _resident(conv_w.shape),
                _resident(conv_b.shape), _resident(w_down.shape)]
    args = [x, gain, w_up, conv_w, conv_b, w_down]
    if final:
        in_specs.append(_resident(final_gain.shape))
        args.append(final_gain)
    return pl.pallas_call(
        functools.partial(_ffn_body, final=final),
        grid=(bsz, s // TOK_TILE),
        in_specs=in_specs,
        out_specs=tile,
        out_shape=jax.ShapeDtypeStruct(x.shape, x.dtype),
        scratch_shapes=[pltpu.VMEM((SUBLANES_V7X, 2 * FFN_DIM), F32),
                        pltpu.VMEM((2, SUBLANES_V7X + TOK_TILE, FFN_CHUNK), F32),
                        pltpu.VMEM((TOK_TILE, FFN_DIM), BF16)],
        compiler_params=_cparams(("arbitrary", "arbitrary"), 56),
        name="conv_ffn",
    )(*args)


def _to_sub(a, dil):
    b, s, c = a.shape
    return a.reshape(b, s // dil, dil, c).transpose(0, 2, 1, 3).reshape(b * dil, s // dil, c)


def _to_sub_t(a, dil):
    b, c, s = a.shape
    return a.reshape(b, c, s // dil, dil).transpose(0, 3, 1, 2).reshape(b * dil, c, s // dil)


def _from_sub(a, b, dil):
    n, length, c = a.shape
    return a.reshape(b, dil, length, c).transpose(0, 2, 1, 3).reshape(b, length * dil, c)


def kernel(x, norm_mix, w_in, attn_sinks, ssm_lambda_re, ssm_lambda_im, ssm_log_dt, ssm_b_re, ssm_b_im,
           ssm_c_re, ssm_c_im, ssm_d, w_glu, b_glu, w_branch_a, w_branch_b, w_branch_c, w_out,
           norm_ffn, w_up, conv_w, conv_b, w_down, norm_final):
    bsz, s, _ = x.shape
    depth = w_in.shape[0]
    assert bsz == SUBLANES_V7X, "the S5 kernel maps the batch onto the sublane axis"
    assert s % TOK_TILE == 0 and s % SSM_T == 0
    assert all((s // dil) % BLOCK == 0 for _, dil in DIL_PATTERNS)

    a_re, a_im, bb_re, bb_im = _ssm_params(ssm_lambda_re, ssm_lambda_im, ssm_log_dt, ssm_b_re, ssm_b_im)
    row = lambda v: v.reshape(1, -1)
    kv_w = B_KV

    for l in range(depth):
        wl = w_in[l]
        w_row = jnp.concatenate([wl[:, IN_U:IN_U + SSM_WIDTH], wl[:, IN_KD:IN_KD + B_KV],
                                 wl[:, IN_KA:IN_KA + A_KV]], axis=1).astype(BF16)
        w_col = jnp.concatenate([wl[:, IN_QA:IN_QA + A_Q], wl[:, IN_QD:IN_QD + B_Q],
                                 wl[:, IN_VD:IN_VD + B_KV], wl[:, IN_VA:IN_VA + A_KV]], axis=1).T.astype(BF16)
        w_gate = wl[:, W_MAIN:].astype(BF16)
        rows, cols = _inproj(x, row(norm_mix[l]), w_row, w_col)

        ya = _banded_attention(cols, COL_QA // A_Q, rows, ROW_KA // A_KV, cols, COL_VA // A_KV,
                               n_kv=SWA_KV_HEADS, rep=SWA_Q_HEADS // SWA_KV_HEADS,
                               max_off=SWA_WINDOW - 1, sinks=attn_sinks[l])

        outs, lses = [], []
        for gi, (window, dil) in enumerate(DIL_PATTERNS):
            q0 = COL_QD + gi * kv_w
            if dil == 1:
                o, lse = _banded_attention(cols, q0 // kv_w, rows, ROW_KD // kv_w, cols, COL_VD // kv_w,
                                           n_kv=DIL_HEADS, rep=1, max_off=window // dil, want_lse=True)
            else:
                sub_k = _to_sub(rows[:, :, ROW_KD:ROW_KD + kv_w], dil)
                sub_qv = _to_sub_t(jnp.concatenate([cols[:, q0:q0 + kv_w, :],
                                                    cols[:, COL_VD:COL_VD + kv_w, :]], axis=1), dil)
                o, lse = _banded_attention(sub_qv, 0, sub_k, 0, sub_qv, 1, n_kv=DIL_HEADS, rep=1,
                                           max_off=window // dil, want_lse=True)
                o, lse = _from_sub(o, bsz, dil), _from_sub(lse, bsz, dil)
            outs.append(o)
            lses.append(lse)

        b_mat, c_mat = _ssm_matrices(bb_re[l], bb_im[l], ssm_c_re[l], ssm_c_im[l])
        bcast = lambda v: jnp.broadcast_to(v[None, :], (SUBLANES_V7X, N_STATE))
        yc = _ssm(rows, ROW_U // SSM_WIDTH, b_mat, bcast(a_re[l]), bcast(a_im[l]), c_mat, row(ssm_d[l]),
                  w_glu[l].astype(BF16), row(b_glu[l]))

        x = _merge(x, ya, outs, lses, yc, row(norm_mix[l]), w_gate, w_branch_a[l].astype(BF16),
                   w_branch_b[l].astype(BF16), w_branch_c[l].astype(BF16), w_out[l].astype(BF16))
        x = _ffn(x, row(norm_ffn[l]), w_up[l].astype(BF16), conv_w[l], row(conv_b[l]),
                 w_down[l].astype(BF16), final_gain=row(norm_final) if l == depth - 1 else None)
    return x
```

```python
import functools

import jax
import jax.numpy as jnp
from jax import lax
from jax.experimental import pallas as pl
from jax.experimental.pallas import tpu as pltpu

F32 = jnp.float32
BF16 = jnp.bfloat16

D_MODEL = 1024
HEAD_DIM = 64
BLOCK = 128
EPS = 1e-6
NEG_INF = -1e30
SWA_Q_HEADS = 8
SWA_KV_HEADS = 2
SWA_WINDOW = 128
DIL_PATTERNS = ((128, 1), (512, 4), (2048, 16))
N_DIL = 3
DIL_HEADS = 4
SSM_GROUP = 16
SSM_GROUPS = 32
SSM_WIDTH = SSM_GROUP * SSM_GROUPS
SSM_STATE = 64
N_STATE = SSM_GROUPS * SSM_STATE
N_BRANCH = 3
FFN_DIM = 2816
CONV_WIDTH = 3

A_Q = SWA_Q_HEADS * HEAD_DIM
A_KV = SWA_KV_HEADS * HEAD_DIM
B_Q = N_DIL * DIL_HEADS * HEAD_DIM
B_KV = DIL_HEADS * HEAD_DIM
GATE_W = N_BRANCH * D_MODEL
W_MAIN = A_Q + 2 * A_KV + B_Q + 2 * B_KV + SSM_WIDTH

IN_QA, IN_KA, IN_VA = 0, A_Q, A_Q + A_KV
IN_QD = A_Q + 2 * A_KV
IN_KD = IN_QD + B_Q
IN_VD = IN_KD + B_KV
IN_U = IN_VD + B_KV

ROW_W = SSM_WIDTH + B_KV + A_KV
ROW_U, ROW_KD, ROW_KA = 0, SSM_WIDTH, SSM_WIDTH + B_KV
COL_W = A_Q + B_KV + B_KV + A_KV
COL_QA, COL_QD, COL_VD, COL_VA = 0, A_Q, A_Q + B_KV, A_Q + 2 * B_KV
COL_Q_ROWS = A_Q + B_KV

DIL_MID, DIL_WIDE = DIL_PATTERNS[1][1], DIL_PATTERNS[2][1]
DIL_RATIO = DIL_WIDE // DIL_MID
DIL_TILE = DIL_WIDE * BLOCK
PERM_CHUNK = BLOCK // DIL_RATIO

SUBLANES_V7X = 8
LANES_V7X = 128
MIB = 1024 * 1024

TOK_TILE = 512
ATTN_TILE = 1024
ATTN_LANES = 1024
SSM_T = 64
SSM_CHUNK = 512
FFN_CHUNK = 256


def _cparams(semantics, vmem_mib):
    return pltpu.CompilerParams(dimension_semantics=semantics, vmem_limit_bytes=vmem_mib * MIB)


def _resident(shape):
    nd = len(shape)
    return pl.BlockSpec(shape, lambda *_: (0,) * nd, pipeline_mode=pl.Buffered(1))


def _rmsnorm(x, g):
    return x * lax.rsqrt(jnp.mean(x * x, axis=-1, keepdims=True) + EPS) * g


def _dot(a, b):
    return jnp.dot(a, b, preferred_element_type=F32)


def _dot_nt(a, b):
    return lax.dot_general(a, b, (((1,), (1,)), ((), ())), preferred_element_type=F32)


def _inproj_body(x_ref, g_ref, wr_ref, wc_ref, row_ref, col_ref):
    h = _rmsnorm(x_ref[...], g_ref[...]).astype(BF16)
    row_ref[...] = _dot(h, wr_ref[...]).astype(BF16)
    col = _dot_nt(wc_ref[...], h)
    col_ref[:COL_Q_ROWS, :] = (col[:COL_Q_ROWS] * (HEAD_DIM ** -0.5)).astype(BF16)
    col_ref[COL_Q_ROWS:, :] = col[COL_Q_ROWS:].astype(BF16)


def _inproj(x, gain, w_row, w_col):
    bsz, s, _ = x.shape
    return pl.pallas_call(
        _inproj_body,
        grid=(bsz, s // TOK_TILE),
        in_specs=[pl.BlockSpec((None, TOK_TILE, D_MODEL), lambda b, i: (b, i, 0)),
                  _resident((1, D_MODEL)), _resident(w_row.shape), _resident(w_col.shape)],
        out_specs=[pl.BlockSpec((None, TOK_TILE, ROW_W), lambda b, i: (b, i, 0)),
                   pl.BlockSpec((None, COL_W, TOK_TILE), lambda b, i: (b, 0, i))],
        out_shape=[jax.ShapeDtypeStruct((bsz, s, ROW_W), BF16),
                   jax.ShapeDtypeStruct((bsz, COL_W, s), BF16)],
        compiler_params=_cparams(("parallel", "parallel"), 40),
        name="inproj",
    )(x, gain, w_row, w_col)


def _dil_proj_body(x_ref, g_ref, wk_ref, wm_ref, ww_ref, km_ref, qvm_ref, kw_ref, qvw_ref, hs_scr, hp_scr):
    n_slab = D_MODEL // LANES_V7X
    scale = HEAD_DIM ** -0.5
    for c in range(DIL_TILE // TOK_TILE):
        rows = slice(c * TOK_TILE, (c + 1) * TOK_TILE)
        h = _rmsnorm(x_ref[rows, :], g_ref[...])
        for s in range(n_slab):
            hs_scr[s, rows, :] = h[:, s * LANES_V7X:(s + 1) * LANES_V7X]
    for r in range(DIL_WIDE):
        piece = jnp.concatenate([hs_scr[s, pl.ds(r, BLOCK, stride=DIL_WIDE), :] for s in range(n_slab)], axis=1)
        hp_scr[r * BLOCK:(r + 1) * BLOCK, :] = piece.astype(BF16)

    group = TOK_TILE // BLOCK
    for r0 in range(0, DIL_WIDE, group):
        hp = hp_scr[r0 * BLOCK:(r0 + group) * BLOCK, :]
        k = _dot(hp, wk_ref[...]).astype(BF16)
        qv = _dot_nt(ww_ref[...], hp)
        for n in range(group):
            cols = slice(n * BLOCK, (n + 1) * BLOCK)
            kw_ref[r0 + n] = k[cols, :]
            qvw_ref[r0 + n, :B_KV, :] = (qv[:B_KV, cols] * scale).astype(BF16)
            qvw_ref[r0 + n, B_KV:, :] = qv[B_KV:, cols].astype(BF16)
    per_res = DIL_TILE // DIL_MID
    for r4 in range(DIL_MID):
        hp = jnp.concatenate(
            [hp_scr[(r4 + DIL_MID * m) * BLOCK + PERM_CHUNK * j:(r4 + DIL_MID * m) * BLOCK + PERM_CHUNK * (j + 1), :]
             for j in range(per_res // BLOCK) for m in range(DIL_RATIO)], axis=0)
        km_ref[r4] = _dot(hp, wk_ref[...]).astype(BF16)
        qv = _dot_nt(wm_ref[...], hp)
        qvm_ref[r4, :B_KV, :] = (qv[:B_KV] * scale).astype(BF16)
        qvm_ref[r4, B_KV:, :] = qv[B_KV:].astype(BF16)


def _dil_proj(x, gain, w_k, w_mid_t, w_wide_t):
    bsz, s, _ = x.shape
    assert s % DIL_TILE == 0 and DIL_TILE % TOK_TILE == 0
    mid_len, wide_len = s // DIL_MID, s // DIL_WIDE
    per_res = DIL_TILE // DIL_MID
    return pl.pallas_call(
        _dil_proj_body,
        grid=(bsz, s // DIL_TILE),
        in_specs=[pl.BlockSpec((None, DIL_TILE, D_MODEL), lambda b, i: (b, i, 0)),
                  _resident(gain.shape), _resident(w_k.shape), _resident(w_mid_t.shape),
                  _resident(w_wide_t.shape)],
        out_specs=[pl.BlockSpec((None, DIL_MID, per_res, B_KV), lambda b, i: (b, 0, i, 0)),
                   pl.BlockSpec((None, DIL_MID, 2 * B_KV, per_res), lambda b, i: (b, 0, 0, i)),
                   pl.BlockSpec((None, DIL_WIDE, BLOCK, B_KV), lambda b, i: (b, 0, i, 0)),
                   pl.BlockSpec((None, DIL_WIDE, 2 * B_KV, BLOCK), lambda b, i: (b, 0, 0, i))],
        out_shape=[jax.ShapeDtypeStruct((bsz, DIL_MID, mid_len, B_KV), BF16),
                   jax.ShapeDtypeStruct((bsz, DIL_MID, 2 * B_KV, mid_len), BF16),
                   jax.ShapeDtypeStruct((bsz, DIL_WIDE, wide_len, B_KV), BF16),
                   jax.ShapeDtypeStruct((bsz, DIL_WIDE, 2 * B_KV, wide_len), BF16)],
        scratch_shapes=[pltpu.VMEM((D_MODEL // LANES_V7X, DIL_TILE, LANES_V7X), F32),
                        pltpu.VMEM((DIL_TILE, D_MODEL), BF16)],
        compiler_params=_cparams(("parallel", "parallel"), 56),
        name="dilated_proj",
    )(x, gain, w_k, w_mid_t, w_wide_t)


def _attn_body(*refs, n_kv, rep, max_off, tq, nseq, use_sink, want_lse, perm_step):
    qt_ref, kp_ref, k_ref, vtp_ref, vt_ref = refs[:5]
    pos = 5
    sink_ref = None
    if use_sink:
        sink_ref = refs[pos]
        pos += 1
    o_ref = refs[pos]
    lse_ref = refs[pos + 1] if want_lse else None

    i = pl.program_id(1)
    kpos = lax.broadcasted_iota(jnp.int32, (2 * BLOCK, BLOCK), 0)
    qidx = lax.broadcasted_iota(jnp.int32, (2 * BLOCK, BLOCK), 1)
    if perm_step:
        chunk = BLOCK // perm_step
        assert chunk & (chunk - 1) == 0 and BLOCK & (BLOCK - 1) == 0
        shift = chunk.bit_length() - 1
        place = lambda r: perm_step * (r & (chunk - 1)) + (r >> shift)
        kpos = (kpos & ~(BLOCK - 1)) + place(kpos & (BLOCK - 1))
        qidx = place(qidx)
    hi = qidx + BLOCK
    lo = hi - max_off
    band_cap = jnp.where(kpos <= hi, jnp.where(kpos >= lo, jnp.inf, NEG_INF), NEG_INF).astype(F32)
    first_lo = jnp.where(i > 0, 0, BLOCK)
    first_cap = jnp.where(kpos >= first_lo, band_cap, NEG_INF)
    zero_q = jnp.zeros((HEAD_DIM, rep * BLOCK), BF16)
    kv_unit = min(n_kv, max(1, ATTN_LANES // (rep * BLOCK)))
    n_heads = n_kv * rep

    blocks = [(sq, j) for sq in range(nseq) for j in range(tq // BLOCK)]
    units = []
    for sq, j in blocks:
        cap = first_cap if j == 0 else band_cap
        cap = jnp.concatenate([cap] * (kv_unit * rep), axis=1)
        cur = slice(j * BLOCK, (j + 1) * BLOCK)
        if j == 0:
            k_prev, vt_prev = kp_ref[sq], vtp_ref[sq]
        else:
            prv = slice((j - 1) * BLOCK, j * BLOCK)
            k_prev, vt_prev = k_ref[sq, prv, :], vt_ref[sq, :, prv]
        kk = jnp.concatenate([k_prev, k_ref[sq, cur, :]], axis=0)
        vvt = jnp.concatenate([vt_prev, vt_ref[sq, :, cur]], axis=1)
        for g0 in range(0, n_kv, kv_unit):
            unit = range(g0, g0 + kv_unit)
            heads = [g * rep + r for g in unit for r in range(rep)]
            qbd = jnp.concatenate(
                [jnp.concatenate(
                    [jnp.concatenate([qt_ref[sq, (g * rep + r) * HEAD_DIM:(g * rep + r + 1) * HEAD_DIM, cur]
                                      for r in range(rep)], axis=1) if g2 == g else zero_q
                     for g2 in range(n_kv)], axis=0)
                 for g in unit], axis=1)
            st = jnp.minimum(_dot(kk, qbd), cap)
            units.append(((sq, j), g0, heads, st, vvt[g0 * HEAD_DIM:(g0 + kv_unit) * HEAD_DIM, :]))

    soft = []
    for _, g0, heads, st, vt_u in units:
        m = jnp.max(st, axis=0, keepdims=True)
        if use_sink:
            sk = jnp.concatenate([jnp.full((1, BLOCK), sink_ref[h], F32) for h in heads], axis=1)
            m = jnp.maximum(m, sk)
        p = jnp.exp(st - m)
        l = jnp.sum(p, axis=0, keepdims=True)
        if use_sink:
            l = l + jnp.exp(sk - m)
        soft.append((p.astype(BF16), 1.0 / l, m + jnp.log(l) if want_lse else None))

    out_t = {blk: [] for blk in blocks}
    lse_t = {blk: [] for blk in blocks}
    for (blk, g0, heads, st, vt_u), (p, inv_l, lse) in zip(units, soft):
        ot_all = _dot(vt_u, p) * inv_l
        for n, h in enumerate(heads):
            g = h // rep
            out_t[blk].append(ot_all[(g - g0) * HEAD_DIM:(g - g0 + 1) * HEAD_DIM, n * BLOCK:(n + 1) * BLOCK])
            if want_lse:
                lse_t[blk].append(jnp.broadcast_to(lse[:, n * BLOCK:(n + 1) * BLOCK], (HEAD_DIM, BLOCK)))
    for sq, j in blocks:
        cur = slice(j * BLOCK, (j + 1) * BLOCK)
        assert len(out_t[sq, j]) == n_heads
        o_ref[sq, cur, :] = jnp.concatenate(out_t[sq, j], axis=0).T.astype(o_ref.dtype)
        if want_lse:
            lse_ref[sq, cur, :] = jnp.concatenate(lse_t[sq, j], axis=0).T


def _banded_attention(qt_arr, q_blk, k_arr, k_blk, vt_arr, v_blk, *, n_kv, rep, max_off, sinks=None,
                      want_lse=False, perm_step=0):
    n, seq_len, _ = k_arr.shape
    tq = min(ATTN_TILE, seq_len)
    nseq = ATTN_TILE // tq
    assert seq_len % tq == 0 and n % nseq == 0
    per = tq // BLOCK
    qw = n_kv * rep * HEAD_DIM
    kw = n_kv * HEAD_DIM
    prev_blk = lambda i: jnp.maximum(i * per - 1, 0)
    in_specs = [pl.BlockSpec((nseq, qw, tq), lambda b, i: (b, q_blk, i)),
                pl.BlockSpec((nseq, BLOCK, kw), lambda b, i: (b, prev_blk(i), k_blk)),
                pl.BlockSpec((nseq, tq, kw), lambda b, i: (b, i, k_blk)),
                pl.BlockSpec((nseq, kw, BLOCK), lambda b, i: (b, v_blk, prev_blk(i))),
                pl.BlockSpec((nseq, kw, tq), lambda b, i: (b, v_blk, i))]
    args = [qt_arr, k_arr, k_arr, vt_arr, vt_arr]
    if sinks is not None:
        in_specs.append(pl.BlockSpec(memory_space=pltpu.SMEM))
        args.append(sinks)
    out_spec = pl.BlockSpec((nseq, tq, qw), lambda b, i: (b, i, 0))
    out_shape = [jax.ShapeDtypeStruct((n, seq_len, qw), BF16)]
    out_specs = [out_spec]
    if want_lse:
        out_shape.append(jax.ShapeDtypeStruct((n, seq_len, qw), F32))
        out_specs.append(out_spec)
    body = functools.partial(_attn_body, n_kv=n_kv, rep=rep, max_off=max_off, tq=tq, nseq=nseq,
                             use_sink=sinks is not None, want_lse=want_lse, perm_step=perm_step)
    res = pl.pallas_call(
        body,
        grid=(n // nseq, seq_len // tq),
        in_specs=in_specs,
        out_specs=out_specs,
        out_shape=out_shape,
        compiler_params=_cparams(("parallel", "parallel"), 48),
        name="banded_attention",
    )(*args)
    return res if want_lse else res[0]


def _ssm_param_body(lr_ref, li_ref, ldt_ref, br_ref, bi_ref, ar_ref, ai_ref, bbr_ref, bbi_ref):
    lr, li = lr_ref[...], li_ref[...]
    dt = jnp.exp(ldt_ref[...])
    mag = jnp.exp(lr * dt)
    ab_re, ab_im = mag * jnp.cos(li * dt), mag * jnp.sin(li * dt)
    nr, ni = ab_re - 1.0, ab_im
    den = lr * lr + li * li
    f_re = (nr * lr + ni * li) / den
    f_im = (ni * lr - nr * li) / den
    br, bi = br_ref[...], bi_ref[...]
    ar_ref[...] = ab_re
    ai_ref[...] = ab_im
    bbr_ref[...] = f_re * br - f_im * bi
    bbi_ref[...] = f_re * bi + f_im * br


def _ssm_params(lam_re, lam_im, log_dt, b_re, b_im):
    nl = lam_re.shape[0]
    rows = nl * SSM_GROUPS * SSM_GROUP
    rep = lambda a: jnp.broadcast_to(a[:, :, None, :], (nl, SSM_GROUPS, SSM_GROUP, SSM_STATE)).reshape(
        rows, SSM_STATE)
    ldt = jnp.broadcast_to(log_dt[:, :, None, None], (nl, SSM_GROUPS, SSM_GROUP, SSM_STATE)).reshape(
        rows, SSM_STATE)
    bt = lambda b: jnp.swapaxes(b, 2, 3).reshape(rows, SSM_STATE)
    shape = jax.ShapeDtypeStruct((rows, SSM_STATE), F32)
    ar, ai, bbr, bbi = pl.pallas_call(
        _ssm_param_body, out_shape=[shape] * 4, name="ssm_params",
    )(rep(lam_re), rep(lam_im), ldt, bt(b_re), bt(b_im))
    pick = lambda a: a.reshape(nl, SSM_GROUPS, SSM_GROUP, SSM_STATE)[:, :, 0, :].reshape(nl, N_STATE)
    unflat = lambda a: a.reshape(nl, SSM_GROUPS, SSM_GROUP, SSM_STATE)
    return pick(ar), pick(ai), unflat(bbr), unflat(bbi)


SSM_HALF_GROUPS = SSM_GROUPS // 2
SSM_HALF_IN = SSM_HALF_GROUPS * SSM_GROUP
SSM_HALF_STATE = SSM_HALF_GROUPS * SSM_STATE


def _ssm_matrices(bb_re, bb_im, c_re, c_im):
    eye = jnp.eye(SSM_HALF_GROUPS, dtype=F32)

    def in_blocks(bb):
        bb = bb.reshape(2, SSM_HALF_GROUPS, SSM_GROUP, SSM_STATE)
        return jnp.einsum("ab,kahp->kahbp", eye, bb).reshape(2, SSM_HALF_IN, SSM_HALF_STATE)

    def out_blocks(c):
        c = c.reshape(2, SSM_HALF_GROUPS, SSM_GROUP, SSM_STATE)
        return jnp.einsum("ab,kahp->kapbh", eye, c).reshape(2, SSM_HALF_STATE, SSM_HALF_IN)

    b_mat = jnp.concatenate([in_blocks(bb_re), in_blocks(bb_im)], axis=2).astype(BF16)
    c_mat = jnp.concatenate([out_blocks(c_re), out_blocks(-c_im)], axis=1).astype(BF16)
    return b_mat, c_mat


def _ssm_body(u_ref, bm_ref, ar_ref, ai_ref, cm_ref, d_ref, wg_ref, bg_ref, o_ref,
              x_scr, sr_scr, si_scr, tb_scr):
    nb = SUBLANES_V7X
    width = 2 * SSM_HALF_STATE
    n_slab = SSM_WIDTH // LANES_V7X

    @pl.when(pl.program_id(0) == 0)
    def _():
        sr_scr[...] = jnp.zeros_like(sr_scr)
        si_scr[...] = jnp.zeros_like(si_scr)

    for b in range(nb):
        ub = u_ref[b].astype(F32)
        for s in range(n_slab):
            tb_scr[s, pl.ds(b, SSM_T, stride=nb), :] = ub[:, s * LANES_V7X:(s + 1) * LANES_V7X]
    u32 = jnp.concatenate([tb_scr[s] for s in range(n_slab)], axis=1)
    u = u32.astype(BF16)
    for half in range(2):
        ucols = slice(half * SSM_HALF_IN, (half + 1) * SSM_HALF_IN)
        x_scr[:, half * width:(half + 1) * width] = _dot(u[:, ucols], bm_ref[half])

    for half in range(2):
        for c in range(SSM_HALF_STATE // SSM_CHUNK):
            re0 = half * width + c * SSM_CHUNK
            im0 = re0 + SSM_HALF_STATE
            st0 = half * SSM_HALF_STATE + c * SSM_CHUNK
            a_re = ar_ref[:, st0:st0 + SSM_CHUNK]
            a_im = ai_ref[:, st0:st0 + SSM_CHUNK]

            def step(t, carry, re0=re0, im0=im0, a_re=a_re, a_im=a_im):
                xr, xi = carry
                rows = pl.ds(pl.multiple_of(t * nb, nb), nb)
                nr = a_re * xr - a_im * xi + x_scr[rows, re0:re0 + SSM_CHUNK]
                ni = a_re * xi + a_im * xr + x_scr[rows, im0:im0 + SSM_CHUNK]
                x_scr[rows, re0:re0 + SSM_CHUNK] = nr
                x_scr[rows, im0:im0 + SSM_CHUNK] = ni
                return nr, ni

            xr, xi = lax.fori_loop(0, SSM_T, step,
                                   (sr_scr[:, st0:st0 + SSM_CHUNK], si_scr[:, st0:st0 + SSM_CHUNK]),
                                   unroll=8)
            sr_scr[:, st0:st0 + SSM_CHUNK] = xr
            si_scr[:, st0:st0 + SSM_CHUNK] = xi

    ys = [_dot(x_scr[:, half * width:(half + 1) * width].astype(BF16), cm_ref[half]) for half in range(2)]
    y = jnp.concatenate(ys, axis=1) + d_ref[...] * u32
    z = jax.nn.gelu(y)
    z = z * jax.nn.sigmoid(_dot(z.astype(BF16), wg_ref[...]) + bg_ref[...])
    for s in range(n_slab):
        tb_scr[s] = z[:, s * LANES_V7X:(s + 1) * LANES_V7X]
    for b in range(nb):
        for s in range(n_slab):
            o_ref[b, :, s * LANES_V7X:(s + 1) * LANES_V7X] = (
                tb_scr[s, pl.ds(b, SSM_T, stride=nb), :].astype(o_ref.dtype))


def _ssm(u_arr, u_blk, b_mat, a_re, a_im, c_mat, d_skip, w_glu, b_glu):
    bsz, s, _ = u_arr.shape
    rows = SSM_T * SUBLANES_V7X
    return pl.pallas_call(
        _ssm_body,
        grid=(s // SSM_T,),
        in_specs=[pl.BlockSpec((bsz, SSM_T, SSM_WIDTH), lambda i: (0, i, u_blk)),
                  _resident(b_mat.shape), _resident(a_re.shape), _resident(a_im.shape),
                  _resident(c_mat.shape), _resident(d_skip.shape), _resident(w_glu.shape),
                  _resident(b_glu.shape)],
        out_specs=pl.BlockSpec((bsz, SSM_T, SSM_WIDTH), lambda i: (0, i, 0)),
        out_shape=jax.ShapeDtypeStruct((bsz, s, SSM_WIDTH), BF16),
        scratch_shapes=[pltpu.VMEM((rows, 2 * N_STATE), F32),
                        pltpu.VMEM((SUBLANES_V7X, N_STATE), F32),
                        pltpu.VMEM((SUBLANES_V7X, N_STATE), F32),
                        pltpu.VMEM((SSM_WIDTH // LANES_V7X, rows, LANES_V7X), F32)],
        compiler_params=_cparams(("arbitrary",), 40),
        name="s5_mixer",
    )(u_arr, b_mat, a_re, a_im, c_mat, d_skip, w_glu, b_glu)


def _merge_body(x_ref, ya_ref, o0_ref, o1_ref, o2_ref, l0_ref, l1_ref, l2_ref, yc_ref,
                gn_ref, wg_ref, wa_ref, wb_ref, wc_ref, wo_ref, out_ref, merged_scr, tok_scr):
    x = x_ref[...]
    h = _rmsnorm(x, gn_ref[...]).astype(BF16)

    def token_order(slot, piece_of):
        n_slab = B_KV // LANES_V7X
        for r in range(DIL_WIDE):
            piece = piece_of(r).astype(F32)
            for s in range(n_slab):
                tok_scr[slot, s, pl.ds(r, PERM_CHUNK, stride=DIL_WIDE), :] = piece[:, s * LANES_V7X:(s + 1) * LANES_V7X]
        return jnp.concatenate([tok_scr[slot, s] for s in range(n_slab)], axis=1)

    mid = lambda ref: (lambda r: ref[r % DIL_MID, (r // DIL_MID) * PERM_CHUNK:(r // DIL_MID + 1) * PERM_CHUNK, :])
    wide = lambda ref: (lambda r: ref[r])
    o1, l1 = token_order(0, mid(o1_ref)), token_order(1, mid(l1_ref))
    o2, l2 = token_order(2, wide(o2_ref)), token_order(3, wide(l2_ref))
    l0 = l0_ref[...]
    top = jnp.maximum(jnp.maximum(l0, l1), l2)
    e0, e1, e2 = jnp.exp(l0 - top), jnp.exp(l1 - top), jnp.exp(l2 - top)
    inv = 1.0 / (e0 + e1 + e2)
    yb = ((e0 * inv) * o0_ref[...].astype(F32) + (e1 * inv) * o1 + (e2 * inv) * o2).astype(BF16)
    ya = ya_ref[...]
    yc = yc_ref[...]
    step = 256
    for c in range(D_MODEL // step):
        cs = slice(c * step, (c + 1) * step)
        gate = lambda k: jax.nn.sigmoid(_dot(h, wg_ref[:, k * D_MODEL + c * step:k * D_MODEL + (c + 1) * step]))
        merged = (gate(0) * _dot(ya, wa_ref[:, cs]) + gate(1) * _dot(yb, wb_ref[:, cs])
                  + gate(2) * _dot(yc, wc_ref[:, cs]))
        merged_scr[:, cs] = merged.astype(BF16)
    out_ref[...] = x + _dot(merged_scr[...], wo_ref[...])


def _merge(x, ya, outs, lses, yc, gain, w_gate, w_a, w_b, w_c, w_out):
    bsz, s, _ = x.shape
    assert TOK_TILE == DIL_WIDE * PERM_CHUNK and TOK_TILE // DIL_MID == BLOCK
    tile = lambda width: pl.BlockSpec((None, TOK_TILE, width), lambda b, i: (b, i, 0))
    mid = pl.BlockSpec((None, DIL_MID, BLOCK, B_KV), lambda b, i: (b, 0, i, 0))
    wide = pl.BlockSpec((None, DIL_WIDE, PERM_CHUNK, B_KV), lambda b, i: (b, 0, i, 0))
    return pl.pallas_call(
        _merge_body,
        grid=(bsz, s // TOK_TILE),
        in_specs=[tile(D_MODEL), tile(A_Q)] + [tile(B_KV), mid, wide] * 2 + [tile(SSM_WIDTH),
                  _resident(gain.shape), _resident(w_gate.shape), _resident(w_a.shape),
                  _resident(w_b.shape), _resident(w_c.shape), _resident(w_out.shape)],
        out_specs=tile(D_MODEL),
        out_shape=jax.ShapeDtypeStruct(x.shape, x.dtype),
        scratch_shapes=[pltpu.VMEM((TOK_TILE, D_MODEL), BF16),
                        pltpu.VMEM((4, B_KV // LANES_V7X, TOK_TILE, LANES_V7X), F32)],
        compiler_params=_cparams(("parallel", "parallel"), 48),
        name="gated_merge",
    )(x, ya, *outs, *lses, yc, gain, w_gate, w_a, w_b, w_c, w_out)


def _ffn_body(*refs, final):
    x_ref, gn_ref, wup_ref, cw_ref, cb_ref, wdn_ref = refs[:6]
    pos = 6
    gf_ref = None
    if final:
        gf_ref = refs[pos]
        pos += 1
    out_ref, carry_scr, work_scr, act_scr = refs[pos:pos + 4]
    pad = SUBLANES_V7X

    @pl.when(pl.program_id(1) == 0)
    def _():
        carry_scr[...] = jnp.zeros_like(carry_scr)

    x = x_ref[...]
    h = _rmsnorm(x, gn_ref[...]).astype(BF16)

    def conv_cols(slot, c0):
        cs = slice(c0, c0 + FFN_CHUNK)
        up = _dot(h, wup_ref[:, cs])
        work_scr[slot, 0:pad, :] = carry_scr[:, cs]
        work_scr[slot, pad:pad + TOK_TILE, :] = up
        carry_scr[:, cs] = up[TOK_TILE - pad:, :]
        acc = cb_ref[:, cs] + cw_ref[CONV_WIDTH - 1:CONV_WIDTH, cs] * up
        for k in range(CONV_WIDTH - 1):
            shift = CONV_WIDTH - 1 - k
            acc = acc + cw_ref[k:k + 1, cs] * work_scr[slot, pad - shift:pad - shift + TOK_TILE, :]
        return acc

    for c in range(FFN_DIM // FFN_CHUNK):
        gate = conv_cols(0, c * FFN_CHUNK)
        val = conv_cols(1, FFN_DIM + c * FFN_CHUNK)
        act_scr[:, c * FFN_CHUNK:(c + 1) * FFN_CHUNK] = (jax.nn.silu(gate) * val).astype(BF16)
    y = x + _dot(act_scr[...], wdn_ref[...])
    if final:
        y = _rmsnorm(y, gf_ref[...])
    out_ref[...] = y


def _ffn(x, gain, w_up, conv_w, conv_b, w_down, final_gain=None):
    bsz, s, _ = x.shape
    final = final_gain is not None
    tile = pl.BlockSpec((None, TOK_TILE, D_MODEL), lambda b, i: (b, i, 0))
    in_specs = [tile, _resident(gain.shape), _resident(w_up.shape), _resident(conv_w.shape),
                _resident(conv_b.shape), _resident(w_down.shape)]
    args = [x, gain, w_up, conv_w, conv_b, w_down]
    if final:
        in_specs.append(_resident(final_gain.shape))
        args.append(final_gain)
    return pl.pallas_call(
        functools.partial(_ffn_body, final=final),
        grid=(bsz, s // TOK_TILE),
        in_specs=in_specs,
        out_specs=tile,
        out_shape=jax.ShapeDtypeStruct(x.shape, x.dtype),
        scratch_shapes=[pltpu.VMEM((SUBLANES_V7X, 2 * FFN_DIM), F32),
                        pltpu.VMEM((2, SUBLANES_V7X + TOK_TILE, FFN_CHUNK), F32),
                        pltpu.VMEM((TOK_TILE, FFN_DIM), BF16)],
        compiler_params=_cparams(("arbitrary", "arbitrary"), 56),
        name="conv_ffn",
    )(*args)


def kernel(x, norm_mix, w_in, attn_sinks, ssm_lambda_re, ssm_lambda_im, ssm_log_dt, ssm_b_re, ssm_b_im,
           ssm_c_re, ssm_c_im, ssm_d, w_glu, b_glu, w_branch_a, w_branch_b, w_branch_c, w_out,
           norm_ffn, w_up, conv_w, conv_b, w_down, norm_final):
    bsz, s, _ = x.shape
    depth = w_in.shape[0]
    assert bsz == SUBLANES_V7X, "the S5 kernel maps the batch onto the sublane axis"
    assert s % TOK_TILE == 0 and s % SSM_T == 0
    assert [d for _, d in DIL_PATTERNS] == [1, DIL_MID, DIL_WIDE]
    assert all(w // d == BLOCK for w, d in DIL_PATTERNS), "every dilated group spans one BLOCK of its sub-sequence"

    a_re, a_im, bb_re, bb_im = _ssm_params(ssm_lambda_re, ssm_lambda_im, ssm_log_dt, ssm_b_re, ssm_b_im)
    row = lambda v: v.reshape(1, -1)
    seqs = lambda a: a.reshape((a.shape[0] * a.shape[1],) + a.shape[2:])
    unseqs = lambda a: a.reshape((bsz, a.shape[0] // bsz) + a.shape[1:])
    dil_attn = functools.partial(_banded_attention, n_kv=DIL_HEADS, rep=1, max_off=BLOCK, want_lse=True)

    for l in range(depth):
        wl = w_in[l]
        cols_of = lambda off, width: wl[:, off:off + width]
        w_kd, w_vd = cols_of(IN_KD, B_KV), cols_of(IN_VD, B_KV)
        w_row = jnp.concatenate([cols_of(IN_U, SSM_WIDTH), w_kd, cols_of(IN_KA, A_KV)], axis=1).astype(BF16)
        w_col = jnp.concatenate([cols_of(IN_QA, A_Q), cols_of(IN_QD, B_KV), w_vd, cols_of(IN_VA, A_KV)],
                                axis=1).T.astype(BF16)
        w_mid_t = jnp.concatenate([cols_of(IN_QD + B_KV, B_KV), w_vd], axis=1).T.astype(BF16)
        w_wide_t = jnp.concatenate([cols_of(IN_QD + 2 * B_KV, B_KV), w_vd], axis=1).T.astype(BF16)
        w_gate = wl[:, W_MAIN:].astype(BF16)
        gain = row(norm_mix[l])
        rows, cols = _inproj(x, gain, w_row, w_col)
        k_mid, qv_mid, k_wide, qv_wide = _dil_proj(x, gain, w_kd.astype(BF16), w_mid_t, w_wide_t)

        ya = _banded_attention(cols, COL_QA // A_Q, rows, ROW_KA // A_KV, cols, COL_VA // A_KV,
                               n_kv=SWA_KV_HEADS, rep=SWA_Q_HEADS // SWA_KV_HEADS,
                               max_off=SWA_WINDOW - 1, sinks=attn_sinks[l])

        o0, l0 = dil_attn(cols, COL_QD // B_KV, rows, ROW_KD // B_KV, cols, COL_VD // B_KV)
        o1, l1 = dil_attn(seqs(qv_mid), 0, seqs(k_mid), 0, seqs(qv_mid), 1, perm_step=DIL_RATIO)
        o2, l2 = dil_attn(seqs(qv_wide), 0, seqs(k_wide), 0, seqs(qv_wide), 1)
        outs = [o0, unseqs(o1), unseqs(o2)]
        lses = [l0, unseqs(l1), unseqs(l2)]

        b_mat, c_mat = _ssm_matrices(bb_re[l], bb_im[l], ssm_c_re[l], ssm_c_im[l])
        bcast = lambda v: jnp.broadcast_to(v[None, :], (SUBLANES_V7X, N_STATE))
        yc = _ssm(rows, ROW_U // SSM_WIDTH, b_mat, bcast(a_re[l]), bcast(a_im[l]), c_mat, row(ssm_d[l]),
                  w_glu[l].astype(BF16), row(b_glu[l]))

        x = _merge(x, ya, outs, lses, yc, row(norm_mix[l]), w_gate, w_branch_a[l].astype(BF16),
                   w_branch_b[l].astype(BF16), w_branch_c[l].astype(BF16), w_out[l].astype(BF16))
        x = _ffn(x, row(norm_ffn[l]), w_up[l].astype(BF16), conv_w[l], row(conv_b[l]),
                 w_down[l].astype(BF16), final_gain=row(norm_final) if l == depth - 1 else None)
    return x
```

```python
import functools
import math

import jax
import jax.numpy as jnp
from jax import lax
from jax.experimental import pallas as pl
from jax.experimental.pallas import tpu as pltpu

F32 = jnp.float32
BF16 = jnp.bfloat16

D_MODEL = 1024
HEAD_DIM = 64
BLOCK = 128
EPS = 1e-6
NEG_INF = -1e30
SWA_Q_HEADS = 8
SWA_KV_HEADS = 2
SWA_WINDOW = 128
DIL_PATTERNS = ((128, 1), (512, 4), (2048, 16))
N_DIL = 3
DIL_HEADS = 4
SSM_GROUP = 16
SSM_GROUPS = 32
SSM_WIDTH = SSM_GROUP * SSM_GROUPS
SSM_STATE = 64
N_STATE = SSM_GROUPS * SSM_STATE
N_BRANCH = 3
FFN_DIM = 2816
CONV_WIDTH = 3

A_Q = SWA_Q_HEADS * HEAD_DIM
A_KV = SWA_KV_HEADS * HEAD_DIM
B_Q = N_DIL * DIL_HEADS * HEAD_DIM
B_KV = DIL_HEADS * HEAD_DIM
GATE_W = N_BRANCH * D_MODEL
W_MAIN = A_Q + 2 * A_KV + B_Q + 2 * B_KV + SSM_WIDTH

IN_QA, IN_KA, IN_VA = 0, A_Q, A_Q + A_KV
IN_QD = A_Q + 2 * A_KV
IN_KD = IN_QD + B_Q
IN_VD = IN_KD + B_KV
IN_U = IN_VD + B_KV

ROW_W = SSM_WIDTH + B_KV + A_KV
ROW_U, ROW_KD, ROW_KA = 0, SSM_WIDTH, SSM_WIDTH + B_KV
COL_W = A_Q + B_KV + B_KV + A_KV
COL_QA, COL_QD, COL_VD, COL_VA = 0, A_Q, A_Q + B_KV, A_Q + 2 * B_KV
COL_Q_ROWS = A_Q + B_KV

LOG2_E = math.log2(math.e)
LN_2 = math.log(2.0)
Q_SCALE = HEAD_DIM ** -0.5 * LOG2_E

DIL_MID, DIL_WIDE = DIL_PATTERNS[1][1], DIL_PATTERNS[2][1]
DIL_RATIO = DIL_WIDE // DIL_MID
DIL_TILE = DIL_WIDE * BLOCK
PERM_CHUNK = BLOCK // DIL_RATIO

SUBLANES_V7X = 8
LANES_V7X = 128
MIB = 1024 * 1024

TOK_TILE = 512
ATTN_TILE = 1024
ATTN_LANES = 1024
SSM_T = 64
SSM_CHUNK = 256
FFN_CHUNK = 256


def _cparams(semantics, vmem_mib):
    return pltpu.CompilerParams(dimension_semantics=semantics, vmem_limit_bytes=vmem_mib * MIB)


def _resident(shape):
    nd = len(shape)
    return pl.BlockSpec(shape, lambda *_: (0,) * nd, pipeline_mode=pl.Buffered(1))


def _rmsnorm(x, g):
    return x * lax.rsqrt(jnp.mean(x * x, axis=-1, keepdims=True) + EPS) * g


def _dot(a, b):
    return jnp.dot(a, b, preferred_element_type=F32)


def _dot_nt(a, b):
    return lax.dot_general(a, b, (((1,), (1,)), ((), ())), preferred_element_type=F32)


def _inproj_body(x_ref, g_ref, wr_ref, wc_ref, row_ref, col_ref):
    h = _rmsnorm(x_ref[...], g_ref[...]).astype(BF16)
    row_ref[...] = _dot(h, wr_ref[...]).astype(BF16)
    col = _dot_nt(wc_ref[...], h)
    col_ref[:COL_Q_ROWS, :] = (col[:COL_Q_ROWS] * Q_SCALE).astype(BF16)
    col_ref[COL_Q_ROWS:, :] = col[COL_Q_ROWS:].astype(BF16)


def _inproj(x, gain, w_row, w_col):
    bsz, s, _ = x.shape
    return pl.pallas_call(
        _inproj_body,
        grid=(bsz, s // TOK_TILE),
        in_specs=[pl.BlockSpec((None, TOK_TILE, D_MODEL), lambda b, i: (b, i, 0)),
                  _resident((1, D_MODEL)), _resident(w_row.shape), _resident(w_col.shape)],
        out_specs=[pl.BlockSpec((None, TOK_TILE, ROW_W), lambda b, i: (b, i, 0)),
                   pl.BlockSpec((None, COL_W, TOK_TILE), lambda b, i: (b, 0, i))],
        out_shape=[jax.ShapeDtypeStruct((bsz, s, ROW_W), BF16),
                   jax.ShapeDtypeStruct((bsz, COL_W, s), BF16)],
        compiler_params=_cparams(("parallel", "parallel"), 40),
        name="inproj",
    )(x, gain, w_row, w_col)


def _dil_proj_body(x_ref, g_ref, wk_ref, wm_ref, ww_ref, km_ref, qvm_ref, kw_ref, qvw_ref, hs_scr, hp_scr):
    n_slab = D_MODEL // LANES_V7X
    scale = Q_SCALE
    for c in range(DIL_TILE // TOK_TILE):
        rows = slice(c * TOK_TILE, (c + 1) * TOK_TILE)
        h = _rmsnorm(x_ref[rows, :], g_ref[...])
        for s in range(n_slab):
            hs_scr[s, rows, :] = h[:, s * LANES_V7X:(s + 1) * LANES_V7X]
    for r in range(DIL_WIDE):
        piece = jnp.concatenate([hs_scr[s, pl.ds(r, BLOCK, stride=DIL_WIDE), :] for s in range(n_slab)], axis=1)
        hp_scr[r * BLOCK:(r + 1) * BLOCK, :] = piece.astype(BF16)

    group = TOK_TILE // BLOCK
    for r0 in range(0, DIL_WIDE, group):
        hp = hp_scr[r0 * BLOCK:(r0 + group) * BLOCK, :]
        k = _dot(hp, wk_ref[...]).astype(BF16)
        qv = _dot_nt(ww_ref[...], hp)
        for n in range(group):
            cols = slice(n * BLOCK, (n + 1) * BLOCK)
            kw_ref[r0 + n] = k[cols, :]
            qvw_ref[r0 + n, :B_KV, :] = (qv[:B_KV, cols] * scale).astype(BF16)
            qvw_ref[r0 + n, B_KV:, :] = qv[B_KV:, cols].astype(BF16)
    per_res = DIL_TILE // DIL_MID
    for r4 in range(DIL_MID):
        hp = jnp.concatenate(
            [hp_scr[(r4 + DIL_MID * m) * BLOCK + PERM_CHUNK * j:(r4 + DIL_MID * m) * BLOCK + PERM_CHUNK * (j + 1), :]
             for j in range(per_res // BLOCK) for m in range(DIL_RATIO)], axis=0)
        km_ref[r4] = _dot(hp, wk_ref[...]).astype(BF16)
        qv = _dot_nt(wm_ref[...], hp)
        qvm_ref[r4, :B_KV, :] = (qv[:B_KV] * scale).astype(BF16)
        qvm_ref[r4, B_KV:, :] = qv[B_KV:].astype(BF16)


def _dil_proj(x, gain, w_k, w_mid_t, w_wide_t):
    bsz, s, _ = x.shape
    assert s % DIL_TILE == 0 and DIL_TILE % TOK_TILE == 0
    mid_len, wide_len = s // DIL_MID, s // DIL_WIDE
    per_res = DIL_TILE // DIL_MID
    return pl.pallas_call(
        _dil_proj_body,
        grid=(bsz, s // DIL_TILE),
        in_specs=[pl.BlockSpec((None, DIL_TILE, D_MODEL), lambda b, i: (b, i, 0)),
                  _resident(gain.shape), _resident(w_k.shape), _resident(w_mid_t.shape),
                  _resident(w_wide_t.shape)],
        out_specs=[pl.BlockSpec((None, DIL_MID, per_res, B_KV), lambda b, i: (b, 0, i, 0)),
                   pl.BlockSpec((None, DIL_MID, 2 * B_KV, per_res), lambda b, i: (b, 0, 0, i)),
                   pl.BlockSpec((None, DIL_WIDE, BLOCK, B_KV), lambda b, i: (b, 0, i, 0)),
                   pl.BlockSpec((None, DIL_WIDE, 2 * B_KV, BLOCK), lambda b, i: (b, 0, 0, i))],
        out_shape=[jax.ShapeDtypeStruct((bsz, DIL_MID, mid_len, B_KV), BF16),
                   jax.ShapeDtypeStruct((bsz, DIL_MID, 2 * B_KV, mid_len), BF16),
                   jax.ShapeDtypeStruct((bsz, DIL_WIDE, wide_len, B_KV), BF16),
                   jax.ShapeDtypeStruct((bsz, DIL_WIDE, 2 * B_KV, wide_len), BF16)],
        scratch_shapes=[pltpu.VMEM((D_MODEL // LANES_V7X, DIL_TILE, LANES_V7X), F32),
                        pltpu.VMEM((DIL_TILE, D_MODEL), BF16)],
        compiler_params=_cparams(("parallel", "parallel"), 56),
        name="dilated_proj",
    )(x, gain, w_k, w_mid_t, w_wide_t)


def _attn_body(*refs, n_kv, rep, max_off, tq, nseq, use_sink, want_lse, perm_step):
    qt_ref, kp_ref, k_ref, vtp_ref, vt_ref = refs[:5]
    pos = 5
    sink_ref = None
    if use_sink:
        sink_ref = refs[pos]
        pos += 1
    o_ref = refs[pos]
    lse_ref = refs[pos + 1] if want_lse else None

    i = pl.program_id(1)
    kpos = lax.broadcasted_iota(jnp.int32, (2 * BLOCK, BLOCK), 0)
    qidx = lax.broadcasted_iota(jnp.int32, (2 * BLOCK, BLOCK), 1)
    if perm_step:
        chunk = BLOCK // perm_step
        assert chunk & (chunk - 1) == 0 and BLOCK & (BLOCK - 1) == 0
        shift = chunk.bit_length() - 1
        place = lambda r: perm_step * (r & (chunk - 1)) + (r >> shift)
        kpos = (kpos & ~(BLOCK - 1)) + place(kpos & (BLOCK - 1))
        qidx = place(qidx)
    hi = qidx + BLOCK
    lo = hi - max_off
    band_cap = jnp.where(kpos <= hi, jnp.where(kpos >= lo, jnp.inf, NEG_INF), NEG_INF).astype(F32)
    first_lo = jnp.where(i > 0, 0, BLOCK)
    first_cap = jnp.where(kpos >= first_lo, band_cap, NEG_INF)
    zero_q = jnp.zeros((HEAD_DIM, rep * BLOCK), BF16)
    kv_unit = min(n_kv, max(1, ATTN_LANES // (rep * BLOCK)))
    n_heads = n_kv * rep

    blocks = [(sq, j) for sq in range(nseq) for j in range(tq // BLOCK)]
    units = []
    for sq, j in blocks:
        cap = first_cap if j == 0 else band_cap
        cap = jnp.concatenate([cap] * (kv_unit * rep), axis=1)
        cur = slice(j * BLOCK, (j + 1) * BLOCK)
        if j == 0:
            k_prev, vt_prev = kp_ref[sq], vtp_ref[sq]
        else:
            prv = slice((j - 1) * BLOCK, j * BLOCK)
            k_prev, vt_prev = k_ref[sq, prv, :], vt_ref[sq, :, prv]
        kk = jnp.concatenate([k_prev, k_ref[sq, cur, :]], axis=0)
        vvt = jnp.concatenate([vt_prev, vt_ref[sq, :, cur]], axis=1)
        for g0 in range(0, n_kv, kv_unit):
            unit = range(g0, g0 + kv_unit)
            heads = [g * rep + r for g in unit for r in range(rep)]
            qbd = jnp.concatenate(
                [jnp.concatenate(
                    [jnp.concatenate([qt_ref[sq, (g * rep + r) * HEAD_DIM:(g * rep + r + 1) * HEAD_DIM, cur]
                                      for r in range(rep)], axis=1) if g2 == g else zero_q
                     for g2 in range(n_kv)], axis=0)
                 for g in unit], axis=1)
            st = jnp.minimum(_dot(kk, qbd), cap)
            units.append(((sq, j), g0, heads, st, vvt[g0 * HEAD_DIM:(g0 + kv_unit) * HEAD_DIM, :]))

    soft = []
    for _, g0, heads, st, vt_u in units:
        m = jnp.max(st, axis=0, keepdims=True)
        sink_p = None
        if use_sink:
            sk = jnp.concatenate([jnp.full((1, BLOCK), sink_ref[h] * LOG2_E, F32) for h in heads], axis=1)
            m = jnp.maximum(m, sk)
            sink_p = jnp.exp2(sk - m)
        soft.append((jnp.exp2(st - m).astype(BF16), m, sink_p))

    out_t = {blk: [] for blk in blocks}
    lse_t = {blk: [] for blk in blocks}
    ones_rows = jnp.ones((SUBLANES_V7X, 2 * BLOCK), BF16)
    for (blk, g0, heads, st, vt_u), (p, m, sink_p) in zip(units, soft):
        pv = _dot(jnp.concatenate([vt_u, ones_rows], axis=0), p)
        l = pv[kv_unit * HEAD_DIM:kv_unit * HEAD_DIM + 1, :]
        if use_sink:
            l = l + sink_p
        ot_all = pv[:kv_unit * HEAD_DIM, :] * (1.0 / l)
        lse = (m + jnp.log2(l)) * LN_2 if want_lse else None
        for n, h in enumerate(heads):
            g = h // rep
            out_t[blk].append(ot_all[(g - g0) * HEAD_DIM:(g - g0 + 1) * HEAD_DIM, n * BLOCK:(n + 1) * BLOCK])
            if want_lse:
                lse_t[blk].append(jnp.broadcast_to(lse[:, n * BLOCK:(n + 1) * BLOCK], (HEAD_DIM, BLOCK)))
    for sq, j in blocks:
        cur = slice(j * BLOCK, (j + 1) * BLOCK)
        assert len(out_t[sq, j]) == n_heads
        o_ref[sq, cur, :] = jnp.concatenate(out_t[sq, j], axis=0).T.astype(o_ref.dtype)
        if want_lse:
            lse_ref[sq, cur, :] = jnp.concatenate(lse_t[sq, j], axis=0).T


def _banded_attention(qt_arr, q_blk, k_arr, k_blk, vt_arr, v_blk, *, n_kv, rep, max_off, sinks=None,
                      want_lse=False, perm_step=0):
    n, seq_len, _ = k_arr.shape
    tq = min(ATTN_TILE, seq_len)
    nseq = ATTN_TILE // tq
    assert seq_len % tq == 0 and n % nseq == 0
    per = tq // BLOCK
    qw = n_kv * rep * HEAD_DIM
    kw = n_kv * HEAD_DIM
    prev_blk = lambda i: jnp.maximum(i * per - 1, 0)
    in_specs = [pl.BlockSpec((nseq, qw, tq), lambda b, i: (b, q_blk, i)),
                pl.BlockSpec((nseq, BLOCK, kw), lambda b, i: (b, prev_blk(i), k_blk)),
                pl.BlockSpec((nseq, tq, kw), lambda b, i: (b, i, k_blk)),
                pl.BlockSpec((nseq, kw, BLOCK), lambda b, i: (b, v_blk, prev_blk(i))),
                pl.BlockSpec((nseq, kw, tq), lambda b, i: (b, v_blk, i))]
    args = [qt_arr, k_arr, k_arr, vt_arr, vt_arr]
    if sinks is not None:
        in_specs.append(pl.BlockSpec(memory_space=pltpu.SMEM))
        args.append(sinks)
    out_spec = pl.BlockSpec((nseq, tq, qw), lambda b, i: (b, i, 0))
    out_shape = [jax.ShapeDtypeStruct((n, seq_len, qw), BF16)]
    out_specs = [out_spec]
    if want_lse:
        out_shape.append(jax.ShapeDtypeStruct((n, seq_len, qw), F32))
        out_specs.append(out_spec)
    body = functools.partial(_attn_body, n_kv=n_kv, rep=rep, max_off=max_off, tq=tq, nseq=nseq,
                             use_sink=sinks is not None, want_lse=want_lse, perm_step=perm_step)
    res = pl.pallas_call(
        body,
        grid=(n // nseq, seq_len // tq),
        in_specs=in_specs,
        out_specs=out_specs,
        out_shape=out_shape,
        compiler_params=_cparams(("parallel", "parallel"), 48),
        name="banded_attention",
    )(*args)
    return res if want_lse else res[0]


def _ssm_param_body(lr_ref, li_ref, ldt_ref, br_ref, bi_ref, ar_ref, ai_ref, bbr_ref, bbi_ref):
    lr, li = lr_ref[...], li_ref[...]
    dt = jnp.exp(ldt_ref[...])
    mag = jnp.exp(lr * dt)
    ab_re, ab_im = mag * jnp.cos(li * dt), mag * jnp.sin(li * dt)
    nr, ni = ab_re - 1.0, ab_im
    den = lr * lr + li * li
    f_re = (nr * lr + ni * li) / den
    f_im = (ni * lr - nr * li) / den
    br, bi = br_ref[...], bi_ref[...]
    ar_ref[...] = ab_re
    ai_ref[...] = ab_im
    bbr_ref[...] = f_re * br - f_im * bi
    bbi_ref[...] = f_re * bi + f_im * br


def _ssm_params(lam_re, lam_im, log_dt, b_re, b_im):
    nl = lam_re.shape[0]
    rows = nl * SSM_GROUPS * SSM_GROUP
    rep = lambda a: jnp.broadcast_to(a[:, :, None, :], (nl, SSM_GROUPS, SSM_GROUP, SSM_STATE)).reshape(
        rows, SSM_STATE)
    ldt = jnp.broadcast_to(log_dt[:, :, None, None], (nl, SSM_GROUPS, SSM_GROUP, SSM_STATE)).reshape(
        rows, SSM_STATE)
    bt = lambda b: jnp.swapaxes(b, 2, 3).reshape(rows, SSM_STATE)
    shape = jax.ShapeDtypeStruct((rows, SSM_STATE), F32)
    ar, ai, bbr, bbi = pl.pallas_call(
        _ssm_param_body, out_shape=[shape] * 4, name="ssm_params",
    )(rep(lam_re), rep(lam_im), ldt, bt(b_re), bt(b_im))
    pick = lambda a: a.reshape(nl, SSM_GROUPS, SSM_GROUP, SSM_STATE)[:, :, 0, :].reshape(nl, N_STATE)
    unflat = lambda a: a.reshape(nl, SSM_GROUPS, SSM_GROUP, SSM_STATE)
    return pick(ar), pick(ai), unflat(bbr), unflat(bbi)


SSM_HALF_GROUPS = SSM_GROUPS // 2
SSM_HALF_IN = SSM_HALF_GROUPS * SSM_GROUP
SSM_HALF_STATE = SSM_HALF_GROUPS * SSM_STATE


def _ssm_matrices(bb_re, bb_im, c_re, c_im):
    eye = jnp.eye(SSM_HALF_GROUPS, dtype=F32)

    def in_blocks(bb):
        bb = bb.reshape(2, SSM_HALF_GROUPS, SSM_GROUP, SSM_STATE)
        return jnp.einsum("ab,kahp->kahbp", eye, bb).reshape(2, SSM_HALF_IN, SSM_HALF_STATE)

    def out_blocks(c):
        c = c.reshape(2, SSM_HALF_GROUPS, SSM_GROUP, SSM_STATE)
        return jnp.einsum("ab,kahp->kapbh", eye, c).reshape(2, SSM_HALF_STATE, SSM_HALF_IN)

    b_mat = jnp.concatenate([in_blocks(bb_re), in_blocks(bb_im)], axis=2).astype(BF16)
    c_mat = jnp.concatenate([out_blocks(c_re), out_blocks(-c_im)], axis=1).astype(BF16)
    return b_mat, c_mat


def _ssm_body(u_ref, bm_ref, ar_ref, ai_ref, cm_ref, d_ref, wg_ref, bg_ref, o_ref,
              x_scr, sr_scr, si_scr, tb_scr):
    nb = SUBLANES_V7X
    width = 2 * SSM_HALF_STATE
    n_slab = SSM_WIDTH // LANES_V7X

    @pl.when(pl.program_id(0) == 0)
    def _():
        sr_scr[...] = jnp.zeros_like(sr_scr)
        si_scr[...] = jnp.zeros_like(si_scr)

    for b in range(nb):
        ub = u_ref[b].astype(F32)
        for s in range(n_slab):
            tb_scr[s, pl.ds(b, SSM_T, stride=nb), :] = ub[:, s * LANES_V7X:(s + 1) * LANES_V7X]
    u32 = jnp.concatenate([tb_scr[s] for s in range(n_slab)], axis=1)
    u = u32.astype(BF16)

    per_half = SSM_HALF_STATE // SSM_CHUNK
    chunks = [(half, c) for half in range(2) for c in range(per_half)]

    def cols_of(half, c):
        re0 = half * width + c * SSM_CHUNK
        return re0, re0 + SSM_HALF_STATE, half * SSM_HALF_STATE + c * SSM_CHUNK

    def input_map(half, c):
        re0, im0, _ = cols_of(half, c)
        uh = u[:, half * SSM_HALF_IN:(half + 1) * SSM_HALF_IN]
        for c0, b0 in ((re0, c * SSM_CHUNK), (im0, SSM_HALF_STATE + c * SSM_CHUNK)):
            x_scr[:, c0:c0 + SSM_CHUNK] = _dot(uh, bm_ref[half, :, b0:b0 + SSM_CHUNK])

    def recurrence(half, c):
        re0, im0, st0 = cols_of(half, c)
        a_re = ar_ref[:, st0:st0 + SSM_CHUNK]
        a_im = ai_ref[:, st0:st0 + SSM_CHUNK]
        xr, xi = sr_scr[:, st0:st0 + SSM_CHUNK], si_scr[:, st0:st0 + SSM_CHUNK]
        for t in range(SSM_T):
            rows = slice(t * nb, (t + 1) * nb)
            xr, xi = (a_re * xr - a_im * xi + x_scr[rows, re0:re0 + SSM_CHUNK],
                      a_re * xi + a_im * xr + x_scr[rows, im0:im0 + SSM_CHUNK])
            x_scr[rows, re0:re0 + SSM_CHUNK] = xr
            x_scr[rows, im0:im0 + SSM_CHUNK] = xi
        sr_scr[:, st0:st0 + SSM_CHUNK] = xr
        si_scr[:, st0:st0 + SSM_CHUNK] = xi

    def output_map(half, c):
        re0, im0, _ = cols_of(half, c)
        acc = None
        for c0, r0 in ((re0, c * SSM_CHUNK), (im0, SSM_HALF_STATE + c * SSM_CHUNK)):
            part = _dot(x_scr[:, c0:c0 + SSM_CHUNK].astype(BF16), cm_ref[half, r0:r0 + SSM_CHUNK, :])
            acc = part if acc is None else acc + part
        return acc

    ys = [None, None]
    n = len(chunks)
    for stage in range(n + 2):
        if stage < n:
            input_map(*chunks[stage])
        if 1 <= stage <= n:
            recurrence(*chunks[stage - 1])
        if stage >= 2:
            half, c = chunks[stage - 2]
            part = output_map(half, c)
            ys[half] = part if ys[half] is None else ys[half] + part
    y = jnp.concatenate(ys, axis=1) + d_ref[...] * u32
    z = jax.nn.gelu(y)
    z = z * jax.nn.sigmoid(_dot(z.astype(BF16), wg_ref[...]) + bg_ref[...])
    for s in range(n_slab):
        tb_scr[s] = z[:, s * LANES_V7X:(s + 1) * LANES_V7X]
    for b in range(nb):
        for s in range(n_slab):
            o_ref[b, :, s * LANES_V7X:(s + 1) * LANES_V7X] = (
                tb_scr[s, pl.ds(b, SSM_T, stride=nb), :].astype(o_ref.dtype))


def _ssm(u_arr, u_blk, b_mat, a_re, a_im, c_mat, d_skip, w_glu, b_glu):
    bsz, s, _ = u_arr.shape
    rows = SSM_T * SUBLANES_V7X
    return pl.pallas_call(
        _ssm_body,
        grid=(s // SSM_T,),
        in_specs=[pl.BlockSpec((bsz, SSM_T, SSM_WIDTH), lambda i: (0, i, u_blk)),
                  _resident(b_mat.shape), _resident(a_re.shape), _resident(a_im.shape),
                  _resident(c_mat.shape), _resident(d_skip.shape), _resident(w_glu.shape),
                  _resident(b_glu.shape)],
        out_specs=pl.BlockSpec((bsz, SSM_T, SSM_WIDTH), lambda i: (0, i, 0)),
        out_shape=jax.ShapeDtypeStruct((bsz, s, SSM_WIDTH), BF16),
        scratch_shapes=[pltpu.VMEM((rows, 2 * N_STATE), F32),
                        pltpu.VMEM((SUBLANES_V7X, N_STATE), F32),
                        pltpu.VMEM((SUBLANES_V7X, N_STATE), F32),
                        pltpu.VMEM((SSM_WIDTH // LANES_V7X, rows, LANES_V7X), F32)],
        compiler_params=_cparams(("arbitrary",), 40),
        name="s5_mixer",
    )(u_arr, b_mat, a_re, a_im, c_mat, d_skip, w_glu, b_glu)


def _merge_body(x_ref, ya_ref, o0_ref, o1_ref, o2_ref, l0_ref, l1_ref, l2_ref, yc_ref,
                gn_ref, wg_ref, wa_ref, wb_ref, wc_ref, wo_ref, out_ref, merged_scr, tok_scr):
    x = x_ref[...]
    h = _rmsnorm(x, gn_ref[...]).astype(BF16)

    def token_order(slot, piece_of):
        n_slab = B_KV // LANES_V7X
        for r in range(DIL_WIDE):
            piece = piece_of(r).astype(F32)
            for s in range(n_slab):
                tok_scr[slot, s, pl.ds(r, PERM_CHUNK, stride=DIL_WIDE), :] = piece[:, s * LANES_V7X:(s + 1) * LANES_V7X]
        return jnp.concatenate([tok_scr[slot, s] for s in range(n_slab)], axis=1)

    mid = lambda ref: (lambda r: ref[r % DIL_MID, (r // DIL_MID) * PERM_CHUNK:(r // DIL_MID + 1) * PERM_CHUNK, :])
    wide = lambda ref: (lambda r: ref[r])
    o1, l1 = token_order(0, mid(o1_ref)), token_order(1, mid(l1_ref))
    o2, l2 = token_order(2, wide(o2_ref)), token_order(3, wide(l2_ref))
    l0 = l0_ref[...]
    top = jnp.maximum(jnp.maximum(l0, l1), l2)
    e0, e1, e2 = jnp.exp(l0 - top), jnp.exp(l1 - top), jnp.exp(l2 - top)
    inv = 1.0 / (e0 + e1 + e2)
    yb = ((e0 * inv) * o0_ref[...].astype(F32) + (e1 * inv) * o1 + (e2 * inv) * o2).astype(BF16)
    ya = ya_ref[...]
    yc = yc_ref[...]
    step = 256
    for c in range(D_MODEL // step):
        cs = slice(c * step, (c + 1) * step)
        gate = lambda k: jax.nn.sigmoid(_dot(h, wg_ref[:, k * D_MODEL + c * step:k * D_MODEL + (c + 1) * step]))
        merged = (gate(0) * _dot(ya, wa_ref[:, cs]) + gate(1) * _dot(yb, wb_ref[:, cs])
                  + gate(2) * _dot(yc, wc_ref[:, cs]))
        merged_scr[:, cs] = merged.astype(BF16)
    out_ref[...] = x + _dot(merged_scr[...], wo_ref[...])


def _merge(x, ya, outs, lses, yc, gain, w_gate, w_a, w_b, w_c, w_out):
    bsz, s, _ = x.shape
    assert TOK_TILE == DIL_WIDE * PERM_CHUNK and TOK_TILE // DIL_MID == BLOCK
    tile = lambda width: pl.BlockSpec((None, TOK_TILE, width), lambda b, i: (b, i, 0))
    mid = pl.BlockSpec((None, DIL_MID, BLOCK, B_KV), lambda b, i: (b, 0, i, 0))
    wide = pl.BlockSpec((None, DIL_WIDE, PERM_CHUNK, B_KV), lambda b, i: (b, 0, i, 0))
    return pl.pallas_call(
        _merge_body,
        grid=(bsz, s // TOK_TILE),
        in_specs=[tile(D_MODEL), tile(A_Q)] + [tile(B_KV), mid, wide] * 2 + [tile(SSM_WIDTH),
                  _resident(gain.shape), _resident(w_gate.shape), _resident(w_a.shape),
                  _resident(w_b.shape), _resident(w_c.shape), _resident(w_out.shape)],
        out_specs=tile(D_MODEL),
        out_shape=jax.ShapeDtypeStruct(x.shape, x.dtype),
        scratch_shapes=[pltpu.VMEM((TOK_TILE, D_MODEL), BF16),
                        pltpu.VMEM((4, B_KV // LANES_V7X, TOK_TILE, LANES_V7X), F32)],
        compiler_params=_cparams(("parallel", "parallel"), 48),
        name="gated_merge",
    )(x, ya, *outs, *lses, yc, gain, w_gate, w_a, w_b, w_c, w_out)


def _ffn_body(*refs, final):
    x_ref, gn_ref, wup_ref, cw_ref, cb_ref, wdn_ref = refs[:6]
    pos = 6
    gf_ref = None
    if final:
        gf_ref = refs[pos]
        pos += 1
    out_ref, carry_scr, work_scr, act_scr = refs[pos:pos + 4]
    pad = SUBLANES_V7X

    @pl.when(pl.program_id(1) == 0)
    def _():
        carry_scr[...] = jnp.zeros_like(carry_scr)

    x = x_ref[...]
    h = _rmsnorm(x, gn_ref[...]).astype(BF16)

    def conv_cols(slot, c0):
        cs = slice(c0, c0 + FFN_CHUNK)
        up = _dot(h, wup_ref[:, cs])
        work_scr[slot, 0:pad, :] = carry_scr[:, cs]
        work_scr[slot, pad:pad + TOK_TILE, :] = up
        carry_scr[:, cs] = up[TOK_TILE - pad:, :]
        acc = cb_ref[:, cs] + cw_ref[CONV_WIDTH - 1:CONV_WIDTH, cs] * up
        for k in range(CONV_WIDTH - 1):
            shift = CONV_WIDTH - 1 - k
            acc = acc + cw_ref[k:k + 1, cs] * work_scr[slot, pad - shift:pad - shift + TOK_TILE, :]
        return acc

    for c in range(FFN_DIM // FFN_CHUNK):
        gate = conv_cols(0, c * FFN_CHUNK)
        val = conv_cols(1, FFN_DIM + c * FFN_CHUNK)
        act_scr[:, c * FFN_CHUNK:(c + 1) * FFN_CHUNK] = (jax.nn.silu(gate) * val).astype(BF16)
    y = x + _dot(act_scr[...], wdn_ref[...])
    if final:
        y = _rmsnorm(y, gf_ref[...])
    out_ref[...] = y


def _ffn(x, gain, w_up, conv_w, conv_b, w_down, final_gain=None):
    bsz, s, _ = x.shape
    final = final_gain is not None
    tile = pl.BlockSpec((None, TOK_TILE, D_MODEL), lambda b, i: (b, i, 0))
    in_specs = [tile, _resident(gain.shape), _resident(w_up.shape), _resident(conv_w.shape),
                _resident(conv_b.shape), _resident(w_down.shape)]
    args = [x, gain, w_up, conv_w, conv_b, w_down]
    if final:
        in_specs.append(_resident(final_gain.shape))
        args.append(final_gain)
    return pl.pallas_call(
        functools.partial(_ffn_body, final=final),
        grid=(bsz, s // TOK_TILE),
        in_specs=in_specs,
        out_specs=tile,
        out_shape=jax.ShapeDtypeStruct(x.shape, x.dtype),
        scratch_shapes=[pltpu.VMEM((SUBLANES_V7X, 2 * FFN_DIM), F32),
                        pltpu.VMEM((2, SUBLANES_V7X + TOK_TILE, FFN_CHUNK), F32),
                        pltpu.VMEM((TOK_TILE, FFN_DIM), BF16)],
        compiler_params=_cparams(("arbitrary", "arbitrary"), 56),
        name="conv_ffn",
    )(*args)


def kernel(x, norm_mix, w_in, attn_sinks, ssm_lambda_re, ssm_lambda_im, ssm_log_dt, ssm_b_re, ssm_b_im,
           ssm_c_re, ssm_c_im, ssm_d, w_glu, b_glu, w_branch_a, w_branch_b, w_branch_c, w_out,
           norm_ffn, w_up, conv_w, conv_b, w_down, norm_final):
    bsz, s, _ = x.shape
    depth = w_in.shape[0]
    assert bsz == SUBLANES_V7X, "the S5 kernel maps the batch onto the sublane axis"
    assert s % TOK_TILE == 0 and s % SSM_T == 0
    assert [d for _, d in DIL_PATTERNS] == [1, DIL_MID, DIL_WIDE]
    assert all(w // d == BLOCK for w, d in DIL_PATTERNS), "every dilated group spans one BLOCK of its sub-sequence"

    a_re, a_im, bb_re, bb_im = _ssm_params(ssm_lambda_re, ssm_lambda_im, ssm_log_dt, ssm_b_re, ssm_b_im)
    row = lambda v: v.reshape(1, -1)
    seqs = lambda a: a.reshape((a.shape[0] * a.shape[1],) + a.shape[2:])
    unseqs = lambda a: a.reshape((bsz, a.shape[0] // bsz) + a.shape[1:])
    dil_attn = functools.partial(_banded_attention, n_kv=DIL_HEADS, rep=1, max_off=BLOCK, want_lse=True)

    for l in range(depth):
        wl = w_in[l]
        cols_of = lambda off, width: wl[:, off:off + width]
        w_kd, w_vd = cols_of(IN_KD, B_KV), cols_of(IN_VD, B_KV)
        w_row = jnp.concatenate([cols_of(IN_U, SSM_WIDTH), w_kd, cols_of(IN_KA, A_KV)], axis=1).astype(BF16)
        w_col = jnp.concatenate([cols_of(IN_QA, A_Q), cols_of(IN_QD, B_KV), w_vd, cols_of(IN_VA, A_KV)],
                                axis=1).T.astype(BF16)
        w_mid_t = jnp.concatenate([cols_of(IN_QD + B_KV, B_KV), w_vd], axis=1).T.astype(BF16)
        w_wide_t = jnp.concatenate([cols_of(IN_QD + 2 * B_KV, B_KV), w_vd], axis=1).T.astype(BF16)
        w_gate = wl[:, W_MAIN:].astype(BF16)
        gain = row(norm_mix[l])
        rows, cols = _inproj(x, gain, w_row, w_col)
        k_mid, qv_mid, k_wide, qv_wide = _dil_proj(x, gain, w_kd.astype(BF16), w_mid_t, w_wide_t)

        ya = _banded_attention(cols, COL_QA // A_Q, rows, ROW_KA // A_KV, cols, COL_VA // A_KV,
                               n_kv=SWA_KV_HEADS, rep=SWA_Q_HEADS // SWA_KV_HEADS,
                               max_off=SWA_WINDOW - 1, sinks=attn_sinks[l])

        o0, l0 = dil_attn(cols, COL_QD // B_KV, rows, ROW_KD // B_KV, cols, COL_VD // B_KV)
        o1, l1 = dil_attn(seqs(qv_mid), 0, seqs(k_mid), 0, seqs(qv_mid), 1, perm_step=DIL_RATIO)
        o2, l2 = dil_attn(seqs(qv_wide), 0, seqs(k_wide), 0, seqs(qv_wide), 1)
        outs = [o0, unseqs(o1), unseqs(o2)]
        lses = [l0, unseqs(l1), unseqs(l2)]

        b_mat, c_mat = _ssm_matrices(bb_re[l], bb_im[l], ssm_c_re[l], ssm_c_im[l])
        bcast = lambda v: jnp.broadcast_to(v[None, :], (SUBLANES_V7X, N_STATE))
        yc = _ssm(rows, ROW_U // SSM_WIDTH, b_mat, bcast(a_re[l]), bcast(a_im[l]), c_mat, row(ssm_d[l]),
                  w_glu[l].astype(BF16), row(b_glu[l]))

        x = _merge(x, ya, outs, lses, yc, row(norm_mix[l]), w_gate, w_branch_a[l].astype(BF16),
                   w_branch_b[l].astype(BF16), w_branch_c[l].astype(BF16), w_out[l].astype(BF16))
        x = _ffn(x, row(norm_ffn[l]), w_up[l].astype(BF16), conv_w[l], row(conv_b[l]),
                 w_down[l].astype(BF16), final_gain=row(norm_final) if l == depth - 1 else None)
    return x
```

```python
import functools
import math

import jax
import jax.numpy as jnp
from jax import lax
from jax.experimental import pallas as pl
from jax.experimental.pallas import tpu as pltpu

F32 = jnp.float32
BF16 = jnp.bfloat16

D_MODEL = 1024
HEAD_DIM = 64
BLOCK = 128
EPS = 1e-6
NEG_INF = -1e30
SWA_Q_HEADS = 8
SWA_KV_HEADS = 2
SWA_WINDOW = 128
DIL_PATTERNS = ((128, 1), (512, 4), (2048, 16))
N_DIL = 3
DIL_HEADS = 4
SSM_GROUP = 16
SSM_GROUPS = 32
SSM_WIDTH = SSM_GROUP * SSM_GROUPS
SSM_STATE = 64
N_STATE = SSM_GROUPS * SSM_STATE
N_BRANCH = 3
FFN_DIM = 2816
CONV_WIDTH = 3

A_Q = SWA_Q_HEADS * HEAD_DIM
A_KV = SWA_KV_HEADS * HEAD_DIM
B_Q = N_DIL * DIL_HEADS * HEAD_DIM
B_KV = DIL_HEADS * HEAD_DIM
GATE_W = N_BRANCH * D_MODEL
W_MAIN = A_Q + 2 * A_KV + B_Q + 2 * B_KV + SSM_WIDTH

IN_QA, IN_KA, IN_VA = 0, A_Q, A_Q + A_KV
IN_QD = A_Q + 2 * A_KV
IN_KD = IN_QD + B_Q
IN_VD = IN_KD + B_KV
IN_U = IN_VD + B_KV

ROW_W = SSM_WIDTH + B_KV + A_KV
ROW_U, ROW_KD, ROW_KA = 0, SSM_WIDTH, SSM_WIDTH + B_KV
COL_W = A_Q + B_KV + B_KV + A_KV
COL_QA, COL_QD, COL_VD, COL_VA = 0, A_Q, A_Q + B_KV, A_Q + 2 * B_KV
COL_Q_ROWS = A_Q + B_KV

LOG2_E = math.log2(math.e)
LN_2 = math.log(2.0)
Q_SCALE = HEAD_DIM ** -0.5 * LOG2_E

DIL_MID, DIL_WIDE = DIL_PATTERNS[1][1], DIL_PATTERNS[2][1]
DIL_RATIO = DIL_WIDE // DIL_MID
DIL_TILE = DIL_WIDE * BLOCK
PERM_CHUNK = BLOCK // DIL_RATIO

SUBLANES_V7X = 8
LANES_V7X = 128
MIB = 1024 * 1024

TOK_TILE = 512
ATTN_TILE = 1024
ATTN_LANES = 1024
SSM_T = 64
SSM_CHUNK = 256
FFN_CHUNK = 256


def _cparams(semantics, vmem_mib):
    return pltpu.CompilerParams(dimension_semantics=semantics, vmem_limit_bytes=vmem_mib * MIB)


def _resident(shape):
    nd = len(shape)
    return pl.BlockSpec(shape, lambda *_: (0,) * nd, pipeline_mode=pl.Buffered(1))


def _rmsnorm(x, g):
    return x * lax.rsqrt(jnp.mean(x * x, axis=-1, keepdims=True) + EPS) * g


def _dot(a, b):
    return jnp.dot(a, b, preferred_element_type=F32)


def _dot_nt(a, b):
    return lax.dot_general(a, b, (((1,), (1,)), ((), ())), preferred_element_type=F32)


def _inproj_body(x_ref, g_ref, wr_ref, wc_ref, row_ref, col_ref):
    h = _rmsnorm(x_ref[...], g_ref[...]).astype(BF16)
    row_ref[...] = _dot(h, wr_ref[...]).astype(BF16)
    col = _dot_nt(wc_ref[...], h)
    col_ref[:COL_Q_ROWS, :] = (col[:COL_Q_ROWS] * Q_SCALE).astype(BF16)
    col_ref[COL_Q_ROWS:, :] = col[COL_Q_ROWS:].astype(BF16)


def _inproj(x, gain, w_row, w_col):
    bsz, s, _ = x.shape
    return pl.pallas_call(
        _inproj_body,
        grid=(bsz, s // TOK_TILE),
        in_specs=[pl.BlockSpec((None, TOK_TILE, D_MODEL), lambda b, i: (b, i, 0)),
                  _resident((1, D_MODEL)), _resident(w_row.shape), _resident(w_col.shape)],
        out_specs=[pl.BlockSpec((None, TOK_TILE, ROW_W), lambda b, i: (b, i, 0)),
                   pl.BlockSpec((None, COL_W, TOK_TILE), lambda b, i: (b, 0, i))],
        out_shape=[jax.ShapeDtypeStruct((bsz, s, ROW_W), BF16),
                   jax.ShapeDtypeStruct((bsz, COL_W, s), BF16)],
        compiler_params=_cparams(("parallel", "parallel"), 40),
        name="inproj",
    )(x, gain, w_row, w_col)


def _dil_proj_body(x_ref, g_ref, wk_ref, wm_ref, ww_ref, km_ref, qvm_ref, kw_ref, qvw_ref, hs_scr, hp_scr):
    n_slab = D_MODEL // LANES_V7X
    scale = Q_SCALE
    for c in range(DIL_TILE // TOK_TILE):
        rows = slice(c * TOK_TILE, (c + 1) * TOK_TILE)
        h = _rmsnorm(x_ref[rows, :], g_ref[...])
        for s in range(n_slab):
            hs_scr[s, rows, :] = h[:, s * LANES_V7X:(s + 1) * LANES_V7X]
    for r in range(DIL_WIDE):
        piece = jnp.concatenate([hs_scr[s, pl.ds(r, BLOCK, stride=DIL_WIDE), :] for s in range(n_slab)], axis=1)
        hp_scr[r * BLOCK:(r + 1) * BLOCK, :] = piece.astype(BF16)

    group = TOK_TILE // BLOCK
    for r0 in range(0, DIL_WIDE, group):
        hp = hp_scr[r0 * BLOCK:(r0 + group) * BLOCK, :]
        k = _dot(hp, wk_ref[...]).astype(BF16)
        qv = _dot_nt(ww_ref[...], hp)
        for n in range(group):
            cols = slice(n * BLOCK, (n + 1) * BLOCK)
            kw_ref[r0 + n] = k[cols, :]
            qvw_ref[r0 + n, :B_KV, :] = (qv[:B_KV, cols] * scale).astype(BF16)
            qvw_ref[r0 + n, B_KV:, :] = qv[B_KV:, cols].astype(BF16)
    per_res = DIL_TILE // DIL_MID
    for r4 in range(DIL_MID):
        hp = jnp.concatenate(
            [hp_scr[(r4 + DIL_MID * m) * BLOCK + PERM_CHUNK * j:(r4 + DIL_MID * m) * BLOCK + PERM_CHUNK * (j + 1), :]
             for j in range(per_res // BLOCK) for m in range(DIL_RATIO)], axis=0)
        km_ref[r4] = _dot(hp, wk_ref[...]).astype(BF16)
        qv = _dot_nt(wm_ref[...], hp)
        qvm_ref[r4, :B_KV, :] = (qv[:B_KV] * scale).astype(BF16)
        qvm_ref[r4, B_KV:, :] = qv[B_KV:].astype(BF16)


def _dil_proj(x, gain, w_k, w_mid_t, w_wide_t):
    bsz, s, _ = x.shape
    assert s % DIL_TILE == 0 and DIL_TILE % TOK_TILE == 0
    mid_len, wide_len = s // DIL_MID, s // DIL_WIDE
    per_res = DIL_TILE // DIL_MID
    return pl.pallas_call(
        _dil_proj_body,
        grid=(bsz, s // DIL_TILE),
        in_specs=[pl.BlockSpec((None, DIL_TILE, D_MODEL), lambda b, i: (b, i, 0)),
                  _resident(gain.shape), _resident(w_k.shape), _resident(w_mid_t.shape),
                  _resident(w_wide_t.shape)],
        out_specs=[pl.BlockSpec((None, DIL_MID, per_res, B_KV), lambda b, i: (b, 0, i, 0)),
                   pl.BlockSpec((None, DIL_MID, 2 * B_KV, per_res), lambda b, i: (b, 0, 0, i)),
                   pl.BlockSpec((None, DIL_WIDE, BLOCK, B_KV), lambda b, i: (b, 0, i, 0)),
                   pl.BlockSpec((None, DIL_WIDE, 2 * B_KV, BLOCK), lambda b, i: (b, 0, 0, i))],
        out_shape=[jax.ShapeDtypeStruct((bsz, DIL_MID, mid_len, B_KV), BF16),
                   jax.ShapeDtypeStruct((bsz, DIL_MID, 2 * B_KV, mid_len), BF16),
                   jax.ShapeDtypeStruct((bsz, DIL_WIDE, wide_len, B_KV), BF16),
                   jax.ShapeDtypeStruct((bsz, DIL_WIDE, 2 * B_KV, wide_len), BF16)],
        scratch_shapes=[pltpu.VMEM((D_MODEL // LANES_V7X, DIL_TILE, LANES_V7X), F32),
                        pltpu.VMEM((DIL_TILE, D_MODEL), BF16)],
        compiler_params=_cparams(("parallel", "parallel"), 56),
        name="dilated_proj",
    )(x, gain, w_k, w_mid_t, w_wide_t)


def _attn_body(*refs, n_kv, rep, max_off, tq, nseq, use_sink, want_lse, perm_step):
    qt_ref, kp_ref, k_ref, vtp_ref, vt_ref = refs[:5]
    pos = 5
    sink_ref = None
    if use_sink:
        sink_ref = refs[pos]
        pos += 1
    o_ref = refs[pos]
    lse_ref = refs[pos + 1] if want_lse else None

    i = pl.program_id(1)
    kpos = lax.broadcasted_iota(jnp.int32, (2 * BLOCK, BLOCK), 0)
    qidx = lax.broadcasted_iota(jnp.int32, (2 * BLOCK, BLOCK), 1)
    if perm_step:
        chunk = BLOCK // perm_step
        assert chunk & (chunk - 1) == 0 and BLOCK & (BLOCK - 1) == 0
        shift = chunk.bit_length() - 1
        place = lambda r: perm_step * (r & (chunk - 1)) + (r >> shift)
        kpos = (kpos & ~(BLOCK - 1)) + place(kpos & (BLOCK - 1))
        qidx = place(qidx)
    hi = qidx + BLOCK
    lo = hi - max_off
    band_cap = jnp.where(kpos <= hi, jnp.where(kpos >= lo, jnp.inf, NEG_INF), NEG_INF).astype(F32)
    first_lo = jnp.where(i > 0, 0, BLOCK)
    first_cap = jnp.where(kpos >= first_lo, band_cap, NEG_INF)
    zero_q = jnp.zeros((HEAD_DIM, rep * BLOCK), BF16)
    kv_unit = min(n_kv, max(1, ATTN_LANES // (rep * BLOCK)))
    n_heads = n_kv * rep
    lse_rows = LANES_V7X // n_heads

    blocks = [(sq, j) for sq in range(nseq) for j in range(tq // BLOCK)]
    units = []
    for sq, j in blocks:
        cap = first_cap if j == 0 else band_cap
        cap = jnp.concatenate([cap] * (kv_unit * rep), axis=1)
        cur = slice(j * BLOCK, (j + 1) * BLOCK)
        if j == 0:
            k_prev, vt_prev = kp_ref[sq], vtp_ref[sq]
        else:
            prv = slice((j - 1) * BLOCK, j * BLOCK)
            k_prev, vt_prev = k_ref[sq, prv, :], vt_ref[sq, :, prv]
        kk = jnp.concatenate([k_prev, k_ref[sq, cur, :]], axis=0)
        vvt = jnp.concatenate([vt_prev, vt_ref[sq, :, cur]], axis=1)
        for g0 in range(0, n_kv, kv_unit):
            unit = range(g0, g0 + kv_unit)
            heads = [g * rep + r for g in unit for r in range(rep)]
            qbd = jnp.concatenate(
                [jnp.concatenate(
                    [jnp.concatenate([qt_ref[sq, (g * rep + r) * HEAD_DIM:(g * rep + r + 1) * HEAD_DIM, cur]
                                      for r in range(rep)], axis=1) if g2 == g else zero_q
                     for g2 in range(n_kv)], axis=0)
                 for g in unit], axis=1)
            st = jnp.minimum(_dot(kk, qbd), cap)
            units.append(((sq, j), g0, heads, st, vvt[g0 * HEAD_DIM:(g0 + kv_unit) * HEAD_DIM, :]))

    soft = []
    for _, g0, heads, st, vt_u in units:
        m = jnp.max(st, axis=0, keepdims=True)
        sink_p = None
        if use_sink:
            sk = jnp.concatenate([jnp.full((1, BLOCK), sink_ref[h] * LOG2_E, F32) for h in heads], axis=1)
            m = jnp.maximum(m, sk)
            sink_p = jnp.exp2(sk - m)
        soft.append((jnp.exp2(st - m).astype(BF16), m, sink_p))

    out_t = {blk: [] for blk in blocks}
    lse_t = {blk: [] for blk in blocks}
    ones_rows = jnp.ones((SUBLANES_V7X, 2 * BLOCK), BF16)
    for (blk, g0, heads, st, vt_u), (p, m, sink_p) in zip(units, soft):
        pv = _dot(jnp.concatenate([vt_u, ones_rows], axis=0), p)
        l = pv[kv_unit * HEAD_DIM:kv_unit * HEAD_DIM + 1, :]
        if use_sink:
            l = l + sink_p
        ot_all = pv[:kv_unit * HEAD_DIM, :] * (1.0 / l)
        lse = (m + jnp.log2(l)) * LN_2 if want_lse else None
        for n, h in enumerate(heads):
            g = h // rep
            out_t[blk].append(ot_all[(g - g0) * HEAD_DIM:(g - g0 + 1) * HEAD_DIM, n * BLOCK:(n + 1) * BLOCK])
            if want_lse:
                lse_t[blk].append(jnp.broadcast_to(lse[:, n * BLOCK:(n + 1) * BLOCK], (lse_rows, BLOCK)))
    for sq, j in blocks:
        cur = slice(j * BLOCK, (j + 1) * BLOCK)
        heads_t = out_t[sq, j]
        assert len(heads_t) == n_heads
        if want_lse:
            heads_t = [ot[part * lse_rows:(part + 1) * lse_rows] for part in range(HEAD_DIM // lse_rows)
                       for ot in heads_t]
            lse_ref[sq, cur, :] = jnp.concatenate(lse_t[sq, j], axis=0).T
        o_ref[sq, cur, :] = jnp.concatenate(heads_t, axis=0).T.astype(o_ref.dtype)


def _banded_attention(qt_arr, q_blk, k_arr, k_blk, vt_arr, v_blk, *, n_kv, rep, max_off, sinks=None,
                      want_lse=False, perm_step=0):
    n, seq_len, _ = k_arr.shape
    tq = min(ATTN_TILE, seq_len)
    nseq = ATTN_TILE // tq
    assert seq_len % tq == 0 and n % nseq == 0
    per = tq // BLOCK
    qw = n_kv * rep * HEAD_DIM
    kw = n_kv * HEAD_DIM
    prev_blk = lambda i: jnp.maximum(i * per - 1, 0)
    in_specs = [pl.BlockSpec((nseq, qw, tq), lambda b, i: (b, q_blk, i)),
                pl.BlockSpec((nseq, BLOCK, kw), lambda b, i: (b, prev_blk(i), k_blk)),
                pl.BlockSpec((nseq, tq, kw), lambda b, i: (b, i, k_blk)),
                pl.BlockSpec((nseq, kw, BLOCK), lambda b, i: (b, v_blk, prev_blk(i))),
                pl.BlockSpec((nseq, kw, tq), lambda b, i: (b, v_blk, i))]
    args = [qt_arr, k_arr, k_arr, vt_arr, vt_arr]
    if sinks is not None:
        in_specs.append(pl.BlockSpec(memory_space=pltpu.SMEM))
        args.append(sinks)
    out_spec = pl.BlockSpec((nseq, tq, qw), lambda b, i: (b, i, 0))
    out_shape = [jax.ShapeDtypeStruct((n, seq_len, qw), BF16)]
    out_specs = [out_spec]
    if want_lse:
        assert LANES_V7X % (n_kv * rep) == 0 and HEAD_DIM % (LANES_V7X // (n_kv * rep)) == 0
        out_shape.append(jax.ShapeDtypeStruct((n, seq_len, LANES_V7X), F32))
        out_specs.append(pl.BlockSpec((nseq, tq, LANES_V7X), lambda b, i: (b, i, 0)))
    body = functools.partial(_attn_body, n_kv=n_kv, rep=rep, max_off=max_off, tq=tq, nseq=nseq,
                             use_sink=sinks is not None, want_lse=want_lse, perm_step=perm_step)
    res = pl.pallas_call(
        body,
        grid=(n // nseq, seq_len // tq),
        in_specs=in_specs,
        out_specs=out_specs,
        out_shape=out_shape,
        compiler_params=_cparams(("parallel", "parallel"), 48),
        name="banded_attention",
    )(*args)
    return res if want_lse else res[0]


def _ssm_param_body(lr_ref, li_ref, ldt_ref, br_ref, bi_ref, ar_ref, ai_ref, bbr_ref, bbi_ref):
    lr, li = lr_ref[...], li_ref[...]
    dt = jnp.exp(ldt_ref[...])
    mag = jnp.exp(lr * dt)
    ab_re, ab_im = mag * jnp.cos(li * dt), mag * jnp.sin(li * dt)
    nr, ni = ab_re - 1.0, ab_im
    den = lr * lr + li * li
    f_re = (nr * lr + ni * li) / den
    f_im = (ni * lr - nr * li) / den
    br, bi = br_ref[...], bi_ref[...]
    ar_ref[...] = ab_re
    ai_ref[...] = ab_im
    bbr_ref[...] = f_re * br - f_im * bi
    bbi_ref[...] = f_re * bi + f_im * br


def _ssm_params(lam_re, lam_im, log_dt, b_re, b_im):
    nl = lam_re.shape[0]
    rows = nl * SSM_GROUPS * SSM_GROUP
    rep = lambda a: jnp.broadcast_to(a[:, :, None, :], (nl, SSM_GROUPS, SSM_GROUP, SSM_STATE)).reshape(
        rows, SSM_STATE)
    ldt = jnp.broadcast_to(log_dt[:, :, None, None], (nl, SSM_GROUPS, SSM_GROUP, SSM_STATE)).reshape(
        rows, SSM_STATE)
    bt = lambda b: jnp.swapaxes(b, 2, 3).reshape(rows, SSM_STATE)
    shape = jax.ShapeDtypeStruct((rows, SSM_STATE), F32)
    ar, ai, bbr, bbi = pl.pallas_call(
        _ssm_param_body, out_shape=[shape] * 4, name="ssm_params",
    )(rep(lam_re), rep(lam_im), ldt, bt(b_re), bt(b_im))
    pick = lambda a: a.reshape(nl, SSM_GROUPS, SSM_GROUP, SSM_STATE)[:, :, 0, :].reshape(nl, N_STATE)
    unflat = lambda a: a.reshape(nl, SSM_GROUPS, SSM_GROUP, SSM_STATE)
    return pick(ar), pick(ai), unflat(bbr), unflat(bbi)


SSM_HALF_GROUPS = SSM_GROUPS // 2
SSM_HALF_IN = SSM_HALF_GROUPS * SSM_GROUP
SSM_HALF_STATE = SSM_HALF_GROUPS * SSM_STATE


def _ssm_matrices(bb_re, bb_im, c_re, c_im):
    eye = jnp.eye(SSM_HALF_GROUPS, dtype=F32)

    def in_blocks(bb):
        bb = bb.reshape(2, SSM_HALF_GROUPS, SSM_GROUP, SSM_STATE)
        return jnp.einsum("ab,kahp->kahbp", eye, bb).reshape(2, SSM_HALF_IN, SSM_HALF_STATE)

    def out_blocks(c):
        c = c.reshape(2, SSM_HALF_GROUPS, SSM_GROUP, SSM_STATE)
        return jnp.einsum("ab,kahp->kapbh", eye, c).reshape(2, SSM_HALF_STATE, SSM_HALF_IN)

    b_mat = jnp.concatenate([in_blocks(bb_re), in_blocks(bb_im)], axis=2).astype(BF16)
    c_mat = jnp.concatenate([out_blocks(c_re), out_blocks(-c_im)], axis=1).astype(BF16)
    return b_mat, c_mat


def _ssm_body(u_ref, bm_ref, ar_ref, ai_ref, cm_ref, d_ref, wg_ref, bg_ref, o_ref,
              x_scr, sr_scr, si_scr, tb_scr):
    nb = SUBLANES_V7X
    width = 2 * SSM_HALF_STATE
    n_slab = SSM_WIDTH // LANES_V7X

    @pl.when(pl.program_id(0) == 0)
    def _():
        sr_scr[...] = jnp.zeros_like(sr_scr)
        si_scr[...] = jnp.zeros_like(si_scr)

    for b in range(nb):
        ub = u_ref[b].astype(F32)
        for s in range(n_slab):
            tb_scr[s, pl.ds(b, SSM_T, stride=nb), :] = ub[:, s * LANES_V7X:(s + 1) * LANES_V7X]
    u32 = jnp.concatenate([tb_scr[s] for s in range(n_slab)], axis=1)
    u = u32.astype(BF16)

    per_half = SSM_HALF_STATE // SSM_CHUNK
    chunks = [(half, c) for half in range(2) for c in range(per_half)]

    def cols_of(half, c):
        re0 = half * width + c * SSM_CHUNK
        return re0, re0 + SSM_HALF_STATE, half * SSM_HALF_STATE + c * SSM_CHUNK

    def input_map(half, c):
        re0, im0, _ = cols_of(half, c)
        uh = u[:, half * SSM_HALF_IN:(half + 1) * SSM_HALF_IN]
        for c0, b0 in ((re0, c * SSM_CHUNK), (im0, SSM_HALF_STATE + c * SSM_CHUNK)):
            x_scr[:, c0:c0 + SSM_CHUNK] = _dot(uh, bm_ref[half, :, b0:b0 + SSM_CHUNK])

    def recurrence(half, c):
        re0, im0, st0 = cols_of(half, c)
        a_re = ar_ref[:, st0:st0 + SSM_CHUNK]
        a_im = ai_ref[:, st0:st0 + SSM_CHUNK]
        xr, xi = sr_scr[:, st0:st0 + SSM_CHUNK], si_scr[:, st0:st0 + SSM_CHUNK]
        for t in range(SSM_T):
            rows = slice(t * nb, (t + 1) * nb)
            xr, xi = (a_re * xr - a_im * xi + x_scr[rows, re0:re0 + SSM_CHUNK],
                      a_re * xi + a_im * xr + x_scr[rows, im0:im0 + SSM_CHUNK])
            x_scr[rows, re0:re0 + SSM_CHUNK] = xr
            x_scr[rows, im0:im0 + SSM_CHUNK] = xi
        sr_scr[:, st0:st0 + SSM_CHUNK] = xr
        si_scr[:, st0:st0 + SSM_CHUNK] = xi

    def output_map(half, c):
        re0, im0, _ = cols_of(half, c)
        acc = None
        for c0, r0 in ((re0, c * SSM_CHUNK), (im0, SSM_HALF_STATE + c * SSM_CHUNK)):
            part = _dot(x_scr[:, c0:c0 + SSM_CHUNK].astype(BF16), cm_ref[half, r0:r0 + SSM_CHUNK, :])
            acc = part if acc is None else acc + part
        return acc

    ys = [None, None]
    n = len(chunks)
    for stage in range(n + 2):
        if stage < n:
            input_map(*chunks[stage])
        if 1 <= stage <= n:
            recurrence(*chunks[stage - 1])
        if stage >= 2:
            half, c = chunks[stage - 2]
            part = output_map(half, c)
            ys[half] = part if ys[half] is None else ys[half] + part
    y = jnp.concatenate(ys, axis=1) + d_ref[...] * u32
    z = jax.nn.gelu(y)
    z = z * jax.nn.sigmoid(_dot(z.astype(BF16), wg_ref[...]) + bg_ref[...])
    for s in range(n_slab):
        tb_scr[s] = z[:, s * LANES_V7X:(s + 1) * LANES_V7X]
    for b in range(nb):
        for s in range(n_slab):
            o_ref[b, :, s * LANES_V7X:(s + 1) * LANES_V7X] = (
                tb_scr[s, pl.ds(b, SSM_T, stride=nb), :].astype(o_ref.dtype))


def _ssm(u_arr, u_blk, b_mat, a_re, a_im, c_mat, d_skip, w_glu, b_glu):
    bsz, s, _ = u_arr.shape
    rows = SSM_T * SUBLANES_V7X
    return pl.pallas_call(
        _ssm_body,
        grid=(s // SSM_T,),
        in_specs=[pl.BlockSpec((bsz, SSM_T, SSM_WIDTH), lambda i: (0, i, u_blk)),
                  _resident(b_mat.shape), _resident(a_re.shape), _resident(a_im.shape),
                  _resident(c_mat.shape), _resident(d_skip.shape), _resident(w_glu.shape),
                  _resident(b_glu.shape)],
        out_specs=pl.BlockSpec((bsz, SSM_T, SSM_WIDTH), lambda i: (0, i, 0)),
        out_shape=jax.ShapeDtypeStruct((bsz, s, SSM_WIDTH), BF16),
        scratch_shapes=[pltpu.VMEM((rows, 2 * N_STATE), F32),
                        pltpu.VMEM((SUBLANES_V7X, N_STATE), F32),
                        pltpu.VMEM((SUBLANES_V7X, N_STATE), F32),
                        pltpu.VMEM((SSM_WIDTH // LANES_V7X, rows, LANES_V7X), F32)],
        compiler_params=_cparams(("arbitrary",), 40),
        name="s5_mixer",
    )(u_arr, b_mat, a_re, a_im, c_mat, d_skip, w_glu, b_glu)


def _merge_body(x_ref, ya_ref, o0_ref, o1_ref, o2_ref, l0_ref, l1_ref, l2_ref, yc_ref,
                gn_ref, wg_ref, wa_ref, wb_ref, wc_ref, wo_ref, out_ref, merged_scr, sub_scr, tok_scr):
    x = x_ref[...]
    h = _rmsnorm(x, gn_ref[...]).astype(BF16)
    n_slab = B_KV // LANES_V7X

    def token_order(slot, piece_of, width):
        for s in range(width // LANES_V7X):
            lanes = slice(s * LANES_V7X, (s + 1) * LANES_V7X)
            for r in range(DIL_WIDE):
                r4, m = r % DIL_MID, r // DIL_MID
                sub_scr[slot, s, r4, pl.ds(m, PERM_CHUNK, stride=DIL_RATIO), :] = piece_of(r)[:, lanes].astype(F32)
            for r4 in range(DIL_MID):
                tok_scr[slot, s, pl.ds(r4, BLOCK, stride=DIL_MID), :] = sub_scr[slot, s, r4]
        return [tok_scr[slot, s] for s in range(width // LANES_V7X)]

    mid = lambda ref: (lambda r: ref[r % DIL_MID, (r // DIL_MID) * PERM_CHUNK:(r // DIL_MID + 1) * PERM_CHUNK, :])
    wide = lambda ref: (lambda r: ref[r])
    o1, (l1,) = token_order(0, mid(o1_ref), B_KV), token_order(1, mid(l1_ref), LANES_V7X)
    o2, (l2,) = token_order(2, wide(o2_ref), B_KV), token_order(3, wide(l2_ref), LANES_V7X)
    l0 = l0_ref[...]
    top = jnp.maximum(jnp.maximum(l0, l1), l2)
    e0, e1, e2 = jnp.exp(l0 - top), jnp.exp(l1 - top), jnp.exp(l2 - top)
    inv = 1.0 / (e0 + e1 + e2)
    w0, w1, w2 = e0 * inv, e1 * inv, e2 * inv
    yb = jnp.concatenate(
        [w0 * o0_ref[:, s * LANES_V7X:(s + 1) * LANES_V7X].astype(F32) + w1 * o1[s] + w2 * o2[s]
         for s in range(n_slab)], axis=1).astype(BF16)
    ya = ya_ref[...]
    yc = yc_ref[...]
    step = 256
    for c in range(D_MODEL // step):
        cs = slice(c * step, (c + 1) * step)
        gate = lambda k: jax.nn.sigmoid(_dot(h, wg_ref[:, k * D_MODEL + c * step:k * D_MODEL + (c + 1) * step]))
        merged = (gate(0) * _dot(ya, wa_ref[:, cs]) + gate(1) * _dot(yb, wb_ref[:, cs])
                  + gate(2) * _dot(yc, wc_ref[:, cs]))
        merged_scr[:, cs] = merged.astype(BF16)
    out_ref[...] = x + _dot(merged_scr[...], wo_ref[...])


def _merge(x, ya, outs, lses, yc, gain, w_gate, w_a, w_b, w_c, w_out):
    bsz, s, _ = x.shape
    assert TOK_TILE == DIL_WIDE * PERM_CHUNK and TOK_TILE // DIL_MID == BLOCK
    tile = lambda width: pl.BlockSpec((None, TOK_TILE, width), lambda b, i: (b, i, 0))
    mid = lambda width: pl.BlockSpec((None, DIL_MID, BLOCK, width), lambda b, i: (b, 0, i, 0))
    wide = lambda width: pl.BlockSpec((None, DIL_WIDE, PERM_CHUNK, width), lambda b, i: (b, 0, i, 0))
    groups = lambda width: [tile(width), mid(width), wide(width)]
    n_slab = B_KV // LANES_V7X
    return pl.pallas_call(
        _merge_body,
        grid=(bsz, s // TOK_TILE),
        in_specs=[tile(D_MODEL), tile(A_Q)] + groups(B_KV) + groups(LANES_V7X) + [tile(SSM_WIDTH),
                  _resident(gain.shape), _resident(w_gate.shape), _resident(w_a.shape),
                  _resident(w_b.shape), _resident(w_c.shape), _resident(w_out.shape)],
        out_specs=tile(D_MODEL),
        out_shape=jax.ShapeDtypeStruct(x.shape, x.dtype),
        scratch_shapes=[pltpu.VMEM((TOK_TILE, D_MODEL), BF16),
                        pltpu.VMEM((4, n_slab, DIL_MID, BLOCK, LANES_V7X), F32),
                        pltpu.VMEM((4, n_slab, TOK_TILE, LANES_V7X), F32)],
        compiler_params=_cparams(("parallel", "parallel"), 48),
        name="gated_merge",
    )(x, ya, *outs, *lses, yc, gain, w_gate, w_a, w_b, w_c, w_out)


def _ffn_body(*refs, final):
    x_ref, gn_ref, wup_ref, cw_ref, cb_ref, wdn_ref = refs[:6]
    pos = 6
    gf_ref = None
    if final:
        gf_ref = refs[pos]
        pos += 1
    out_ref, carry_scr, work_scr, act_scr = refs[pos:pos + 4]
    pad = SUBLANES_V7X

    @pl.when(pl.program_id(1) == 0)
    def _():
        carry_scr[...] = jnp.zeros_like(carry_scr)

    x = x_ref[...]
    h = _rmsnorm(x, gn_ref[...]).astype(BF16)

    def conv_cols(slot, c0):
        cs = slice(c0, c0 + FFN_CHUNK)
        up = _dot(h, wup_ref[:, cs])
        work_scr[slot, 0:pad, :] = carry_scr[:, cs]
        work_scr[slot, pad:pad + TOK_TILE, :] = up
        carry_scr[:, cs] = up[TOK_TILE - pad:, :]
        acc = cb_ref[:, cs] + cw_ref[CONV_WIDTH - 1:CONV_WIDTH, cs] * up
        for k in range(CONV_WIDTH - 1):
            shift = CONV_WIDTH - 1 - k
            acc = acc + cw_ref[k:k + 1, cs] * work_scr[slot, pad - shift:pad - shift + TOK_TILE, :]
        return acc

    for c in range(FFN_DIM // FFN_CHUNK):
        gate = conv_cols(0, c * FFN_CHUNK)
        val = conv_cols(1, FFN_DIM + c * FFN_CHUNK)
        act_scr[:, c * FFN_CHUNK:(c + 1) * FFN_CHUNK] = (jax.nn.silu(gate) * val).astype(BF16)
    y = x + _dot(act_scr[...], wdn_ref[...])
    if final:
        y = _rmsnorm(y, gf_ref[...])
    out_ref[...] = y


def _ffn(x, gain, w_up, conv_w, conv_b, w_down, final_gain=None):
    bsz, s, _ = x.shape
    final = final_gain is not None
    tile = pl.BlockSpec((None, TOK_TILE, D_MODEL), lambda b, i: (b, i, 0))
    in_specs = [tile, _resident(gain.shape), _resident(w_up.shape), _resident(conv_w.shape),
                _resident(conv_b.shape), _resident(w_down.shape)]
    args = [x, gain, w_up, conv_w, conv_b, w_down]
    if final:
        in_specs.append(_resident(final_gain.shape))
        args.append(final_gain)
    return pl.pallas_call(
        functools.partial(_ffn_body, final=final),
        grid=(bsz, s // TOK_TILE),
        in_specs=in_specs,
        out_specs=tile,
        out_shape=jax.ShapeDtypeStruct(x.shape, x.dtype),
        scratch_shapes=[pltpu.VMEM((SUBLANES_V7X, 2 * FFN_DIM), F32),
                        pltpu.VMEM((2, SUBLANES_V7X + TOK_TILE, FFN_CHUNK), F32),
                        pltpu.VMEM((TOK_TILE, FFN_DIM), BF16)],
        compiler_params=_cparams(("arbitrary", "arbitrary"), 56),
        name="conv_ffn",
    )(*args)


def kernel(x, norm_mix, w_in, attn_sinks, ssm_lambda_re, ssm_lambda_im, ssm_log_dt, ssm_b_re, ssm_b_im,
           ssm_c_re, ssm_c_im, ssm_d, w_glu, b_glu, w_branch_a, w_branch_b, w_branch_c, w_out,
           norm_ffn, w_up, conv_w, conv_b, w_down, norm_final):
    bsz, s, _ = x.shape
    depth = w_in.shape[0]
    assert bsz == SUBLANES_V7X, "the S5 kernel maps the batch onto the sublane axis"
    assert s % TOK_TILE == 0 and s % SSM_T == 0
    assert [d for _, d in DIL_PATTERNS] == [1, DIL_MID, DIL_WIDE]
    assert all(w // d == BLOCK for w, d in DIL_PATTERNS), "every dilated group spans one BLOCK of its sub-sequence"

    a_re, a_im, bb_re, bb_im = _ssm_params(ssm_lambda_re, ssm_lambda_im, ssm_log_dt, ssm_b_re, ssm_b_im)
    row = lambda v: v.reshape(1, -1)
    seqs = lambda a: a.reshape((a.shape[0] * a.shape[1],) + a.shape[2:])
    unseqs = lambda a: a.reshape((bsz, a.shape[0] // bsz) + a.shape[1:])
    dil_attn = functools.partial(_banded_attention, n_kv=DIL_HEADS, rep=1, max_off=BLOCK, want_lse=True)

    for l in range(depth):
        wl = w_in[l]
        cols_of = lambda off, width: wl[:, off:off + width]
        w_kd, w_vd = cols_of(IN_KD, B_KV), cols_of(IN_VD, B_KV)
        w_row = jnp.concatenate([cols_of(IN_U, SSM_WIDTH), w_kd, cols_of(IN_KA, A_KV)], axis=1).astype(BF16)
        w_col = jnp.concatenate([cols_of(IN_QA, A_Q), cols_of(IN_QD, B_KV), w_vd, cols_of(IN_VA, A_KV)],
                                axis=1).T.astype(BF16)
        w_mid_t = jnp.concatenate([cols_of(IN_QD + B_KV, B_KV), w_vd], axis=1).T.astype(BF16)
        w_wide_t = jnp.concatenate([cols_of(IN_QD + 2 * B_KV, B_KV), w_vd], axis=1).T.astype(BF16)
        w_gate = wl[:, W_MAIN:].astype(BF16)
        gain = row(norm_mix[l])
        rows, cols = _inproj(x, gain, w_row, w_col)
        k_mid, qv_mid, k_wide, qv_wide = _dil_proj(x, gain, w_kd.astype(BF16), w_mid_t, w_wide_t)

        ya = _banded_attention(cols, COL_QA // A_Q, rows, ROW_KA // A_KV, cols, COL_VA // A_KV,
                               n_kv=SWA_KV_HEADS, rep=SWA_Q_HEADS // SWA_KV_HEADS,
                               max_off=SWA_WINDOW - 1, sinks=attn_sinks[l])

        o0, l0 = dil_attn(cols, COL_QD // B_KV, rows, ROW_KD // B_KV, cols, COL_VD // B_KV)
        o1, l1 = dil_attn(seqs(qv_mid), 0, seqs(k_mid), 0, seqs(qv_mid), 1, perm_step=DIL_RATIO)
        o2, l2 = dil_attn(seqs(qv_wide), 0, seqs(k_wide), 0, seqs(qv_wide), 1)
        outs = [o0, unseqs(o1), unseqs(o2)]
        lses = [l0, unseqs(l1), unseqs(l2)]

        b_mat, c_mat = _ssm_matrices(bb_re[l], bb_im[l], ssm_c_re[l], ssm_c_im[l])
        bcast = lambda v: jnp.broadcast_to(v[None, :], (SUBLANES_V7X, N_STATE))
        yc = _ssm(rows, ROW_U // SSM_WIDTH, b_mat, bcast(a_re[l]), bcast(a_im[l]), c_mat, row(ssm_d[l]),
                  w_glu[l].astype(BF16), row(b_glu[l]))

        lane_w = LANES_V7X // DIL_HEADS
        w_b = w_branch_b[l].reshape(DIL_HEADS, HEAD_DIM // lane_w, lane_w, D_MODEL).transpose(1, 0, 2, 3)
        x = _merge(x, ya, outs, lses, yc, row(norm_mix[l]), w_gate, w_branch_a[l].astype(BF16),
                   w_b.reshape(B_KV, D_MODEL).astype(BF16), w_branch_c[l].astype(BF16), w_out[l].astype(BF16))
        x = _ffn(x, row(norm_ffn[l]), w_up[l].astype(BF16), conv_w[l], row(conv_b[l]),
                 w_down[l].astype(BF16), final_gain=row(norm_final) if l == depth - 1 else None)
    return x
```

```python
import functools
import math

import jax
import jax.numpy as jnp
from jax import lax
from jax.experimental import pallas as pl
from jax.experimental.pallas import tpu as pltpu

F32 = jnp.float32
BF16 = jnp.bfloat16

D_MODEL = 1024
HEAD_DIM = 64
BLOCK = 128
EPS = 1e-6
NEG_INF = -1e30
SWA_Q_HEADS = 8
SWA_KV_HEADS = 2
SWA_WINDOW = 128
DIL_PATTERNS = ((128, 1), (512, 4), (2048, 16))
N_DIL = 3
DIL_HEADS = 4
SSM_GROUP = 16
SSM_GROUPS = 32
SSM_WIDTH = SSM_GROUP * SSM_GROUPS
SSM_STATE = 64
N_STATE = SSM_GROUPS * SSM_STATE
N_BRANCH = 3
FFN_DIM = 2816
CONV_WIDTH = 3

A_Q = SWA_Q_HEADS * HEAD_DIM
A_KV = SWA_KV_HEADS * HEAD_DIM
B_Q = N_DIL * DIL_HEADS * HEAD_DIM
B_KV = DIL_HEADS * HEAD_DIM
GATE_W = N_BRANCH * D_MODEL
W_MAIN = A_Q + 2 * A_KV + B_Q + 2 * B_KV + SSM_WIDTH

IN_QA, IN_KA, IN_VA = 0, A_Q, A_Q + A_KV
IN_QD = A_Q + 2 * A_KV
IN_KD = IN_QD + B_Q
IN_VD = IN_KD + B_KV
IN_U = IN_VD + B_KV

ROW_W = SSM_WIDTH + B_KV + A_KV
ROW_U, ROW_KD, ROW_KA = 0, SSM_WIDTH, SSM_WIDTH + B_KV
COL_W = A_Q + B_KV + B_KV + A_KV
COL_QA, COL_QD, COL_VD, COL_VA = 0, A_Q, A_Q + B_KV, A_Q + 2 * B_KV
COL_Q_ROWS = A_Q + B_KV

LOG2_E = math.log2(math.e)
LN_2 = math.log(2.0)
Q_SCALE = HEAD_DIM ** -0.5 * LOG2_E

DIL_MID, DIL_WIDE = DIL_PATTERNS[1][1], DIL_PATTERNS[2][1]
DIL_RATIO = DIL_WIDE // DIL_MID
DIL_TILE = DIL_WIDE * BLOCK
PERM_CHUNK = BLOCK // DIL_RATIO

SUBLANES_V7X = 8
LANES_V7X = 128
MIB = 1024 * 1024

TOK_TILE = 512
ATTN_TILE = 1024
ATTN_LANES = 1024
SSM_T = 128
SSM_CHUNK = 256
FFN_CHUNK = 256


def _cparams(semantics, vmem_mib):
    return pltpu.CompilerParams(dimension_semantics=semantics, vmem_limit_bytes=vmem_mib * MIB)


def _resident(shape):
    nd = len(shape)
    return pl.BlockSpec(shape, lambda *_: (0,) * nd, pipeline_mode=pl.Buffered(1))


def _rmsnorm(x, g):
    return x * lax.rsqrt(jnp.mean(x * x, axis=-1, keepdims=True) + EPS) * g


def _dot(a, b):
    return jnp.dot(a, b, preferred_element_type=F32)


def _dot_nt(a, b):
    return lax.dot_general(a, b, (((1,), (1,)), ((), ())), preferred_element_type=F32)


def _inproj_body(x_ref, g_ref, wr_ref, wc_ref, row_ref, col_ref):
    h = _rmsnorm(x_ref[...], g_ref[...]).astype(BF16)
    row_ref[...] = _dot(h, wr_ref[...]).astype(BF16)
    col = _dot_nt(wc_ref[...], h)
    col_ref[:COL_Q_ROWS, :] = (col[:COL_Q_ROWS] * Q_SCALE).astype(BF16)
    col_ref[COL_Q_ROWS:, :] = col[COL_Q_ROWS:].astype(BF16)


def _inproj(x, gain, w_row, w_col):
    bsz, s, _ = x.shape
    return pl.pallas_call(
        _inproj_body,
        grid=(bsz, s // TOK_TILE),
        in_specs=[pl.BlockSpec((None, TOK_TILE, D_MODEL), lambda b, i: (b, i, 0)),
                  _resident((1, D_MODEL)), _resident(w_row.shape), _resident(w_col.shape)],
        out_specs=[pl.BlockSpec((None, TOK_TILE, ROW_W), lambda b, i: (b, i, 0)),
                   pl.BlockSpec((None, COL_W, TOK_TILE), lambda b, i: (b, 0, i))],
        out_shape=[jax.ShapeDtypeStruct((bsz, s, ROW_W), BF16),
                   jax.ShapeDtypeStruct((bsz, COL_W, s), BF16)],
        compiler_params=_cparams(("parallel", "parallel"), 40),
        name="inproj",
    )(x, gain, w_row, w_col)


def _dil_proj_body(x_ref, g_ref, wk_ref, wm_ref, ww_ref, km_ref, qvm_ref, kw_ref, qvw_ref,
                   hs_scr, sub_scr, hp_scr):
    n_slab = D_MODEL // LANES_V7X
    scale = Q_SCALE
    for c in range(DIL_TILE // TOK_TILE):
        rows = slice(c * TOK_TILE, (c + 1) * TOK_TILE)
        h = _rmsnorm(x_ref[rows, :], g_ref[...])
        for s in range(n_slab):
            hs_scr[s, rows, :] = h[:, s * LANES_V7X:(s + 1) * LANES_V7X]
    per_mid = DIL_TILE // DIL_MID
    for r4 in range(DIL_MID):
        for s in range(n_slab):
            sub_scr[s] = hs_scr[s, pl.ds(r4, per_mid, stride=DIL_MID), :]
        for m in range(DIL_RATIO):
            r = r4 + DIL_MID * m
            for s in range(n_slab):
                hp_scr[r * BLOCK:(r + 1) * BLOCK, s * LANES_V7X:(s + 1) * LANES_V7X] = (
                    sub_scr[s, pl.ds(m, BLOCK, stride=DIL_RATIO), :].astype(BF16))

    group = TOK_TILE // BLOCK
    for r0 in range(0, DIL_WIDE, group):
        hp = hp_scr[r0 * BLOCK:(r0 + group) * BLOCK, :]
        k = _dot(hp, wk_ref[...]).astype(BF16)
        qv = _dot_nt(ww_ref[...], hp)
        for n in range(group):
            cols = slice(n * BLOCK, (n + 1) * BLOCK)
            kw_ref[r0 + n] = k[cols, :]
            qvw_ref[r0 + n, :B_KV, :] = (qv[:B_KV, cols] * scale).astype(BF16)
            qvw_ref[r0 + n, B_KV:, :] = qv[B_KV:, cols].astype(BF16)
    per_res = DIL_TILE // DIL_MID
    for r4 in range(DIL_MID):
        hp = jnp.concatenate(
            [hp_scr[(r4 + DIL_MID * m) * BLOCK + PERM_CHUNK * j:(r4 + DIL_MID * m) * BLOCK + PERM_CHUNK * (j + 1), :]
             for j in range(per_res // BLOCK) for m in range(DIL_RATIO)], axis=0)
        km_ref[r4] = _dot(hp, wk_ref[...]).astype(BF16)
        qv = _dot_nt(wm_ref[...], hp)
        qvm_ref[r4, :B_KV, :] = (qv[:B_KV] * scale).astype(BF16)
        qvm_ref[r4, B_KV:, :] = qv[B_KV:].astype(BF16)


def _dil_proj(x, gain, w_k, w_mid_t, w_wide_t):
    bsz, s, _ = x.shape
    assert s % DIL_TILE == 0 and DIL_TILE % TOK_TILE == 0
    mid_len, wide_len = s // DIL_MID, s // DIL_WIDE
    per_res = DIL_TILE // DIL_MID
    return pl.pallas_call(
        _dil_proj_body,
        grid=(bsz, s // DIL_TILE),
        in_specs=[pl.BlockSpec((None, DIL_TILE, D_MODEL), lambda b, i: (b, i, 0)),
                  _resident(gain.shape), _resident(w_k.shape), _resident(w_mid_t.shape),
                  _resident(w_wide_t.shape)],
        out_specs=[pl.BlockSpec((None, DIL_MID, per_res, B_KV), lambda b, i: (b, 0, i, 0)),
                   pl.BlockSpec((None, DIL_MID, 2 * B_KV, per_res), lambda b, i: (b, 0, 0, i)),
                   pl.BlockSpec((None, DIL_WIDE, BLOCK, B_KV), lambda b, i: (b, 0, i, 0)),
                   pl.BlockSpec((None, DIL_WIDE, 2 * B_KV, BLOCK), lambda b, i: (b, 0, 0, i))],
        out_shape=[jax.ShapeDtypeStruct((bsz, DIL_MID, mid_len, B_KV), BF16),
                   jax.ShapeDtypeStruct((bsz, DIL_MID, 2 * B_KV, mid_len), BF16),
                   jax.ShapeDtypeStruct((bsz, DIL_WIDE, wide_len, B_KV), BF16),
                   jax.ShapeDtypeStruct((bsz, DIL_WIDE, 2 * B_KV, wide_len), BF16)],
        scratch_shapes=[pltpu.VMEM((D_MODEL // LANES_V7X, DIL_TILE, LANES_V7X), F32),
                        pltpu.VMEM((D_MODEL // LANES_V7X, DIL_TILE // DIL_MID, LANES_V7X), F32),
                        pltpu.VMEM((DIL_TILE, D_MODEL), BF16)],
        compiler_params=_cparams(("parallel", "parallel"), 56),
        name="dilated_proj",
    )(x, gain, w_k, w_mid_t, w_wide_t)


def _attn_body(*refs, n_kv, rep, max_off, tq, nseq, use_sink, want_lse, perm_step):
    qt_ref, kp_ref, k_ref, vtp_ref, vt_ref = refs[:5]
    pos = 5
    sink_ref = None
    if use_sink:
        sink_ref = refs[pos]
        pos += 1
    o_ref = refs[pos]
    lse_ref = refs[pos + 1] if want_lse else None

    i = pl.program_id(1)
    kpos = lax.broadcasted_iota(jnp.int32, (2 * BLOCK, BLOCK), 0)
    qidx = lax.broadcasted_iota(jnp.int32, (2 * BLOCK, BLOCK), 1)
    if perm_step:
        chunk = BLOCK // perm_step
        assert chunk & (chunk - 1) == 0 and BLOCK & (BLOCK - 1) == 0
        shift = chunk.bit_length() - 1
        place = lambda r: perm_step * (r & (chunk - 1)) + (r >> shift)
        kpos = (kpos & ~(BLOCK - 1)) + place(kpos & (BLOCK - 1))
        qidx = place(qidx)
    hi = qidx + BLOCK
    lo = hi - max_off
    band_cap = jnp.where(kpos <= hi, jnp.where(kpos >= lo, jnp.inf, NEG_INF), NEG_INF).astype(F32)
    first_lo = jnp.where(i > 0, 0, BLOCK)
    first_cap = jnp.where(kpos >= first_lo, band_cap, NEG_INF)
    zero_q = jnp.zeros((HEAD_DIM, rep * BLOCK), BF16)
    kv_unit = min(n_kv, max(1, ATTN_LANES // (rep * BLOCK)))
    n_heads = n_kv * rep
    lse_rows = LANES_V7X // n_heads

    blocks = [(sq, j) for sq in range(nseq) for j in range(tq // BLOCK)]
    units = []
    for sq, j in blocks:
        cap = first_cap if j == 0 else band_cap
        cap = jnp.concatenate([cap] * (kv_unit * rep), axis=1)
        cur = slice(j * BLOCK, (j + 1) * BLOCK)
        if j == 0:
            k_prev, vt_prev = kp_ref[sq], vtp_ref[sq]
        else:
            prv = slice((j - 1) * BLOCK, j * BLOCK)
            k_prev, vt_prev = k_ref[sq, prv, :], vt_ref[sq, :, prv]
        kk = jnp.concatenate([k_prev, k_ref[sq, cur, :]], axis=0)
        vvt = jnp.concatenate([vt_prev, vt_ref[sq, :, cur]], axis=1)
        for g0 in range(0, n_kv, kv_unit):
            unit = range(g0, g0 + kv_unit)
            heads = [g * rep + r for g in unit for r in range(rep)]
            qbd = jnp.concatenate(
                [jnp.concatenate(
                    [jnp.concatenate([qt_ref[sq, (g * rep + r) * HEAD_DIM:(g * rep + r + 1) * HEAD_DIM, cur]
                                      for r in range(rep)], axis=1) if g2 == g else zero_q
                     for g2 in range(n_kv)], axis=0)
                 for g in unit], axis=1)
            st = jnp.minimum(_dot(kk, qbd), cap)
            units.append(((sq, j), g0, heads, st, vvt[g0 * HEAD_DIM:(g0 + kv_unit) * HEAD_DIM, :]))

    soft = []
    for _, g0, heads, st, vt_u in units:
        m = jnp.max(st, axis=0, keepdims=True)
        sink_p = None
        if use_sink:
            sk = jnp.concatenate([jnp.full((1, BLOCK), sink_ref[h] * LOG2_E, F32) for h in heads], axis=1)
            m = jnp.maximum(m, sk)
            sink_p = jnp.exp2(sk - m)
        soft.append((jnp.exp2(st - m).astype(BF16), m, sink_p))

    out_t = {blk: [] for blk in blocks}
    lse_t = {blk: [] for blk in blocks}
    ones_rows = jnp.ones((SUBLANES_V7X, 2 * BLOCK), BF16)
    for (blk, g0, heads, st, vt_u), (p, m, sink_p) in zip(units, soft):
        pv = _dot(jnp.concatenate([vt_u, ones_rows], axis=0), p)
        l = pv[kv_unit * HEAD_DIM:kv_unit * HEAD_DIM + 1, :]
        if use_sink:
            l = l + sink_p
        ot_all = pv[:kv_unit * HEAD_DIM, :] * (1.0 / l)
        lse = (m + jnp.log2(l)) * LN_2 if want_lse else None
        for n, h in enumerate(heads):
            g = h // rep
            out_t[blk].append(ot_all[(g - g0) * HEAD_DIM:(g - g0 + 1) * HEAD_DIM, n * BLOCK:(n + 1) * BLOCK])
            if want_lse:
                lse_t[blk].append(jnp.broadcast_to(lse[:, n * BLOCK:(n + 1) * BLOCK], (lse_rows, BLOCK)))
    for sq, j in blocks:
        cur = slice(j * BLOCK, (j + 1) * BLOCK)
        heads_t = out_t[sq, j]
        assert len(heads_t) == n_heads
        if want_lse:
            heads_t = [ot[part * lse_rows:(part + 1) * lse_rows] for part in range(HEAD_DIM // lse_rows)
                       for ot in heads_t]
            lse_ref[sq, cur, :] = jnp.concatenate(lse_t[sq, j], axis=0).T
        o_ref[sq, cur, :] = jnp.concatenate(heads_t, axis=0).T.astype(o_ref.dtype)


def _banded_attention(qt_arr, q_blk, k_arr, k_blk, vt_arr, v_blk, *, n_kv, rep, max_off, sinks=None,
                      want_lse=False, perm_step=0):
    n, seq_len, _ = k_arr.shape
    tq = min(ATTN_TILE, seq_len)
    nseq = ATTN_TILE // tq
    assert seq_len % tq == 0 and n % nseq == 0
    per = tq // BLOCK
    qw = n_kv * rep * HEAD_DIM
    kw = n_kv * HEAD_DIM
    prev_blk = lambda i: jnp.maximum(i * per - 1, 0)
    in_specs = [pl.BlockSpec((nseq, qw, tq), lambda b, i: (b, q_blk, i)),
                pl.BlockSpec((nseq, BLOCK, kw), lambda b, i: (b, prev_blk(i), k_blk)),
                pl.BlockSpec((nseq, tq, kw), lambda b, i: (b, i, k_blk)),
                pl.BlockSpec((nseq, kw, BLOCK), lambda b, i: (b, v_blk, prev_blk(i))),
                pl.BlockSpec((nseq, kw, tq), lambda b, i: (b, v_blk, i))]
    args = [qt_arr, k_arr, k_arr, vt_arr, vt_arr]
    if sinks is not None:
        in_specs.append(pl.BlockSpec(memory_space=pltpu.SMEM))
        args.append(sinks)
    out_spec = pl.BlockSpec((nseq, tq, qw), lambda b, i: (b, i, 0))
    out_shape = [jax.ShapeDtypeStruct((n, seq_len, qw), BF16)]
    out_specs = [out_spec]
    if want_lse:
        assert LANES_V7X % (n_kv * rep) == 0 and HEAD_DIM % (LANES_V7X // (n_kv * rep)) == 0
        out_shape.append(jax.ShapeDtypeStruct((n, seq_len, LANES_V7X), F32))
        out_specs.append(pl.BlockSpec((nseq, tq, LANES_V7X), lambda b, i: (b, i, 0)))
    body = functools.partial(_attn_body, n_kv=n_kv, rep=rep, max_off=max_off, tq=tq, nseq=nseq,
                             use_sink=sinks is not None, want_lse=want_lse, perm_step=perm_step)
    res = pl.pallas_call(
        body,
        grid=(n // nseq, seq_len // tq),
        in_specs=in_specs,
        out_specs=out_specs,
        out_shape=out_shape,
        compiler_params=_cparams(("parallel", "parallel"), 48),
        name="banded_attention",
    )(*args)
    return res if want_lse else res[0]


def _ssm_param_body(lr_ref, li_ref, ldt_ref, br_ref, bi_ref, ar_ref, ai_ref, bbr_ref, bbi_ref):
    lr, li = lr_ref[...], li_ref[...]
    dt = jnp.exp(ldt_ref[...])
    mag = jnp.exp(lr * dt)
    ab_re, ab_im = mag * jnp.cos(li * dt), mag * jnp.sin(li * dt)
    nr, ni = ab_re - 1.0, ab_im
    den = lr * lr + li * li
    f_re = (nr * lr + ni * li) / den
    f_im = (ni * lr - nr * li) / den
    br, bi = br_ref[...], bi_ref[...]
    ar_ref[...] = ab_re
    ai_ref[...] = ab_im
    bbr_ref[...] = f_re * br - f_im * bi
    bbi_ref[...] = f_re * bi + f_im * br


def _ssm_params(lam_re, lam_im, log_dt, b_re, b_im):
    nl = lam_re.shape[0]
    rows = nl * SSM_GROUPS * SSM_GROUP
    rep = lambda a: jnp.broadcast_to(a[:, :, None, :], (nl, SSM_GROUPS, SSM_GROUP, SSM_STATE)).reshape(
        rows, SSM_STATE)
    ldt = jnp.broadcast_to(log_dt[:, :, None, None], (nl, SSM_GROUPS, SSM_GROUP, SSM_STATE)).reshape(
        rows, SSM_STATE)
    bt = lambda b: jnp.swapaxes(b, 2, 3).reshape(rows, SSM_STATE)
    shape = jax.ShapeDtypeStruct((rows, SSM_STATE), F32)
    ar, ai, bbr, bbi = pl.pallas_call(
        _ssm_param_body, out_shape=[shape] * 4, name="ssm_params",
    )(rep(lam_re), rep(lam_im), ldt, bt(b_re), bt(b_im))
    pick = lambda a: a.reshape(nl, SSM_GROUPS, SSM_GROUP, SSM_STATE)[:, :, 0, :].reshape(nl, N_STATE)
    unflat = lambda a: a.reshape(nl, SSM_GROUPS, SSM_GROUP, SSM_STATE)
    return pick(ar), pick(ai), unflat(bbr), unflat(bbi)


SSM_HALF_GROUPS = SSM_GROUPS // 2
SSM_HALF_IN = SSM_HALF_GROUPS * SSM_GROUP
SSM_HALF_STATE = SSM_HALF_GROUPS * SSM_STATE


def _ssm_matrices(bb_re, bb_im, c_re, c_im):
    eye = jnp.eye(SSM_HALF_GROUPS, dtype=F32)

    def in_blocks(bb):
        bb = bb.reshape(2, SSM_HALF_GROUPS, SSM_GROUP, SSM_STATE)
        return jnp.einsum("ab,kahp->kahbp", eye, bb).reshape(2, SSM_HALF_IN, SSM_HALF_STATE)

    def out_blocks(c):
        c = c.reshape(2, SSM_HALF_GROUPS, SSM_GROUP, SSM_STATE)
        return jnp.einsum("ab,kahp->kapbh", eye, c).reshape(2, SSM_HALF_STATE, SSM_HALF_IN)

    b_mat = jnp.concatenate([in_blocks(bb_re), in_blocks(bb_im)], axis=2).astype(BF16)
    c_mat = jnp.concatenate([out_blocks(c_re), out_blocks(-c_im)], axis=1).astype(BF16)
    return b_mat, c_mat


def _ssm_body(u_ref, bm_ref, ar_ref, ai_ref, cm_ref, d_ref, wg_ref, bg_ref, o_ref,
              x_scr, sr_scr, si_scr, tb_scr):
    nb = SUBLANES_V7X
    width = 2 * SSM_HALF_STATE
    n_slab = SSM_WIDTH // LANES_V7X

    @pl.when(pl.program_id(0) == 0)
    def _():
        sr_scr[...] = jnp.zeros_like(sr_scr)
        si_scr[...] = jnp.zeros_like(si_scr)

    for b in range(nb):
        ub = u_ref[b].astype(F32)
        for s in range(n_slab):
            tb_scr[s, pl.ds(b, SSM_T, stride=nb), :] = ub[:, s * LANES_V7X:(s + 1) * LANES_V7X]
    u32 = jnp.concatenate([tb_scr[s] for s in range(n_slab)], axis=1)
    u = u32.astype(BF16)

    per_half = SSM_HALF_STATE // SSM_CHUNK
    chunks = [(half, c) for half in range(2) for c in range(per_half)]

    def cols_of(half, c):
        re0 = half * width + c * SSM_CHUNK
        return re0, re0 + SSM_HALF_STATE, half * SSM_HALF_STATE + c * SSM_CHUNK

    def input_map(half, c):
        re0, im0, _ = cols_of(half, c)
        uh = u[:, half * SSM_HALF_IN:(half + 1) * SSM_HALF_IN]
        for c0, b0 in ((re0, c * SSM_CHUNK), (im0, SSM_HALF_STATE + c * SSM_CHUNK)):
            x_scr[:, c0:c0 + SSM_CHUNK] = _dot(uh, bm_ref[half, :, b0:b0 + SSM_CHUNK])

    def recurrence(half, c):
        re0, im0, st0 = cols_of(half, c)
        a_re = ar_ref[:, st0:st0 + SSM_CHUNK]
        a_im = ai_ref[:, st0:st0 + SSM_CHUNK]
        xr, xi = sr_scr[:, st0:st0 + SSM_CHUNK], si_scr[:, st0:st0 + SSM_CHUNK]
        for t in range(SSM_T):
            rows = slice(t * nb, (t + 1) * nb)
            xr, xi = (a_re * xr - a_im * xi + x_scr[rows, re0:re0 + SSM_CHUNK],
                      a_re * xi + a_im * xr + x_scr[rows, im0:im0 + SSM_CHUNK])
            x_scr[rows, re0:re0 + SSM_CHUNK] = xr
            x_scr[rows, im0:im0 + SSM_CHUNK] = xi
        sr_scr[:, st0:st0 + SSM_CHUNK] = xr
        si_scr[:, st0:st0 + SSM_CHUNK] = xi

    def output_map(half, c):
        re0, im0, _ = cols_of(half, c)
        acc = None
        for c0, r0 in ((re0, c * SSM_CHUNK), (im0, SSM_HALF_STATE + c * SSM_CHUNK)):
            part = _dot(x_scr[:, c0:c0 + SSM_CHUNK].astype(BF16), cm_ref[half, r0:r0 + SSM_CHUNK, :])
            acc = part if acc is None else acc + part
        return acc

    ys = [None, None]
    n = len(chunks)
    for stage in range(n + 2):
        if stage < n:
            input_map(*chunks[stage])
        if 1 <= stage <= n:
            recurrence(*chunks[stage - 1])
        if stage >= 2:
            half, c = chunks[stage - 2]
            part = output_map(half, c)
            ys[half] = part if ys[half] is None else ys[half] + part
    y = jnp.concatenate(ys, axis=1) + d_ref[...] * u32
    z = jax.nn.gelu(y)
    z = z * jax.nn.sigmoid(_dot(z.astype(BF16), wg_ref[...]) + bg_ref[...])
    for s in range(n_slab):
        tb_scr[s] = z[:, s * LANES_V7X:(s + 1) * LANES_V7X]
    for b in range(nb):
        for s in range(n_slab):
            o_ref[b, :, s * LANES_V7X:(s + 1) * LANES_V7X] = (
                tb_scr[s, pl.ds(b, SSM_T, stride=nb), :].astype(o_ref.dtype))


def _ssm(u_arr, u_blk, b_mat, a_re, a_im, c_mat, d_skip, w_glu, b_glu):
    bsz, s, _ = u_arr.shape
    rows = SSM_T * SUBLANES_V7X
    return pl.pallas_call(
        _ssm_body,
        grid=(s // SSM_T,),
        in_specs=[pl.BlockSpec((bsz, SSM_T, SSM_WIDTH), lambda i: (0, i, u_blk)),
                  _resident(b_mat.shape), _resident(a_re.shape), _resident(a_im.shape),
                  _resident(c_mat.shape), _resident(d_skip.shape), _resident(w_glu.shape),
                  _resident(b_glu.shape)],
        out_specs=pl.BlockSpec((bsz, SSM_T, SSM_WIDTH), lambda i: (0, i, 0)),
        out_shape=jax.ShapeDtypeStruct((bsz, s, SSM_WIDTH), BF16),
        scratch_shapes=[pltpu.VMEM((rows, 2 * N_STATE), F32),
                        pltpu.VMEM((SUBLANES_V7X, N_STATE), F32),
                        pltpu.VMEM((SUBLANES_V7X, N_STATE), F32),
                        pltpu.VMEM((SSM_WIDTH // LANES_V7X, rows, LANES_V7X), F32)],
        compiler_params=_cparams(("arbitrary",), 40),
        name="s5_mixer",
    )(u_arr, b_mat, a_re, a_im, c_mat, d_skip, w_glu, b_glu)


def _merge_body(x_ref, ya_ref, o0_ref, o1_ref, o2_ref, l0_ref, l1_ref, l2_ref, yc_ref,
                gn_ref, wg_ref, wa_ref, wb_ref, wc_ref, wo_ref, out_ref, merged_scr, sub_scr, tok_scr):
    x = x_ref[...]
    h = _rmsnorm(x, gn_ref[...]).astype(BF16)
    n_slab = B_KV // LANES_V7X

    def token_order(slot, piece_of, width):
        for s in range(width // LANES_V7X):
            lanes = slice(s * LANES_V7X, (s + 1) * LANES_V7X)
            for r in range(DIL_WIDE):
                r4, m = r % DIL_MID, r // DIL_MID
                sub_scr[slot, s, r4, pl.ds(m, PERM_CHUNK, stride=DIL_RATIO), :] = piece_of(r)[:, lanes].astype(F32)
            for r4 in range(DIL_MID):
                tok_scr[slot, s, pl.ds(r4, BLOCK, stride=DIL_MID), :] = sub_scr[slot, s, r4]
        return [tok_scr[slot, s] for s in range(width // LANES_V7X)]

    mid = lambda ref: (lambda r: ref[r % DIL_MID, (r // DIL_MID) * PERM_CHUNK:(r // DIL_MID + 1) * PERM_CHUNK, :])
    wide = lambda ref: (lambda r: ref[r])
    o1, (l1,) = token_order(0, mid(o1_ref), B_KV), token_order(1, mid(l1_ref), LANES_V7X)
    o2, (l2,) = token_order(2, wide(o2_ref), B_KV), token_order(3, wide(l2_ref), LANES_V7X)
    l0 = l0_ref[...]
    top = jnp.maximum(jnp.maximum(l0, l1), l2)
    e0, e1, e2 = jnp.exp(l0 - top), jnp.exp(l1 - top), jnp.exp(l2 - top)
    inv = 1.0 / (e0 + e1 + e2)
    w0, w1, w2 = e0 * inv, e1 * inv, e2 * inv
    yb = jnp.concatenate(
        [w0 * o0_ref[:, s * LANES_V7X:(s + 1) * LANES_V7X].astype(F32) + w1 * o1[s] + w2 * o2[s]
         for s in range(n_slab)], axis=1).astype(BF16)
    ya = ya_ref[...]
    yc = yc_ref[...]
    step = 256
    for c in range(D_MODEL // step):
        cs = slice(c * step, (c + 1) * step)
        gate = lambda k: jax.nn.sigmoid(_dot(h, wg_ref[:, k * D_MODEL + c * step:k * D_MODEL + (c + 1) * step]))
        merged = (gate(0) * _dot(ya, wa_ref[:, cs]) + gate(1) * _dot(yb, wb_ref[:, cs])
                  + gate(2) * _dot(yc, wc_ref[:, cs]))
        merged_scr[:, cs] = merged.astype(BF16)
    out_ref[...] = x + _dot(merged_scr[...], wo_ref[...])


def _merge(x, ya, outs, lses, yc, gain, w_gate, w_a, w_b, w_c, w_out):
    bsz, s, _ = x.shape
    assert TOK_TILE == DIL_WIDE * PERM_CHUNK and TOK_TILE // DIL_MID == BLOCK
    tile = lambda width: pl.BlockSpec((None, TOK_TILE, width), lambda b, i: (b, i, 0))
    mid = lambda width: pl.BlockSpec((None, DIL_MID, BLOCK, width), lambda b, i: (b, 0, i, 0))
    wide = lambda width: pl.BlockSpec((None, DIL_WIDE, PERM_CHUNK, width), lambda b, i: (b, 0, i, 0))
    groups = lambda width: [tile(width), mid(width), wide(width)]
    n_slab = B_KV // LANES_V7X
    return pl.pallas_call(
        _merge_body,
        grid=(bsz, s // TOK_TILE),
        in_specs=[tile(D_MODEL), tile(A_Q)] + groups(B_KV) + groups(LANES_V7X) + [tile(SSM_WIDTH),
                  _resident(gain.shape), _resident(w_gate.shape), _resident(w_a.shape),
                  _resident(w_b.shape), _resident(w_c.shape), _resident(w_out.shape)],
        out_specs=tile(D_MODEL),
        out_shape=jax.ShapeDtypeStruct(x.shape, x.dtype),
        scratch_shapes=[pltpu.VMEM((TOK_TILE, D_MODEL), BF16),
                        pltpu.VMEM((4, n_slab, DIL_MID, BLOCK, LANES_V7X), F32),
                        pltpu.VMEM((4, n_slab, TOK_TILE, LANES_V7X), F32)],
        compiler_params=_cparams(("parallel", "parallel"), 48),
        name="gated_merge",
    )(x, ya, *outs, *lses, yc, gain, w_gate, w_a, w_b, w_c, w_out)


def _ffn_body(*refs, final):
    x_ref, gn_ref, wup_ref, cw_ref, cb_ref, wdn_ref = refs[:6]
    pos = 6
    gf_ref = None
    if final:
        gf_ref = refs[pos]
        pos += 1
    out_ref, carry_scr, work_scr, act_scr = refs[pos:pos + 4]
    pad = SUBLANES_V7X

    @pl.when(pl.program_id(1) == 0)
    def _():
        carry_scr[...] = jnp.zeros_like(carry_scr)

    x = x_ref[...]
    h = _rmsnorm(x, gn_ref[...]).astype(BF16)

    def conv_cols(slot, c0):
        cs = slice(c0, c0 + FFN_CHUNK)
        up = _dot(h, wup_ref[:, cs])
        work_scr[slot, 0:pad, :] = carry_scr[:, cs]
        work_scr[slot, pad:pad + TOK_TILE, :] = up
        carry_scr[:, cs] = up[TOK_TILE - pad:, :]
        acc = cb_ref[:, cs] + cw_ref[CONV_WIDTH - 1:CONV_WIDTH, cs] * up
        for k in range(CONV_WIDTH - 1):
            shift = CONV_WIDTH - 1 - k
            acc = acc + cw_ref[k:k + 1, cs] * work_scr[slot, pad - shift:pad - shift + TOK_TILE, :]
        return acc

    for c in range(FFN_DIM // FFN_CHUNK):
        gate = conv_cols(0, c * FFN_CHUNK)
        val = conv_cols(1, FFN_DIM + c * FFN_CHUNK)
        act_scr[:, c * FFN_CHUNK:(c + 1) * FFN_CHUNK] = (jax.nn.silu(gate) * val).astype(BF16)
    y = x + _dot(act_scr[...], wdn_ref[...])
    if final:
        y = _rmsnorm(y, gf_ref[...])
    out_ref[...] = y


def _ffn(x, gain, w_up, conv_w, conv_b, w_down, final_gain=None):
    bsz, s, _ = x.shape
    final = final_gain is not None
    tile = pl.BlockSpec((None, TOK_TILE, D_MODEL), lambda b, i: (b, i, 0))
    in_specs = [tile, _resident(gain.shape), _resident(w_up.shape), _resident(conv_w.shape),
                _resident(conv_b.shape), _resident(w_down.shape)]
    args = [x, gain, w_up, conv_w, conv_b, w_down]
    if final:
        in_specs.append(_resident(final_gain.shape))
        args.append(final_gain)
    return pl.pallas_call(
        functools.partial(_ffn_body, final=final),
        grid=(bsz, s // TOK_TILE),
        in_specs=in_specs,
        out_specs=tile,
        out_shape=jax.ShapeDtypeStruct(x.shape, x.dtype),
        scratch_shapes=[pltpu.VMEM((SUBLANES_V7X, 2 * FFN_DIM), F32),
                        pltpu.VMEM((2, SUBLANES_V7X + TOK_TILE, FFN_CHUNK), F32),
                        pltpu.VMEM((TOK_TILE, FFN_DIM), BF16)],
        compiler_params=_cparams(("arbitrary", "arbitrary"), 56),
        name="conv_ffn",
    )(*args)


def kernel(x, norm_mix, w_in, attn_sinks, ssm_lambda_re, ssm_lambda_im, ssm_log_dt, ssm_b_re, ssm_b_im,
           ssm_c_re, ssm_c_im, ssm_d, w_glu, b_glu, w_branch_a, w_branch_b, w_branch_c, w_out,
           norm_ffn, w_up, conv_w, conv_b, w_down, norm_final):
    bsz, s, _ = x.shape
    depth = w_in.shape[0]
    assert bsz == SUBLANES_V7X, "the S5 kernel maps the batch onto the sublane axis"
    assert s % TOK_TILE == 0 and s % SSM_T == 0
    assert [d for _, d in DIL_PATTERNS] == [1, DIL_MID, DIL_WIDE]
    assert all(w // d == BLOCK for w, d in DIL_PATTERNS), "every dilated group spans one BLOCK of its sub-sequence"

    a_re, a_im, bb_re, bb_im = _ssm_params(ssm_lambda_re, ssm_lambda_im, ssm_log_dt, ssm_b_re, ssm_b_im)
    row = lambda v: v.reshape(1, -1)
    seqs = lambda a: a.reshape((a.shape[0] * a.shape[1],) + a.shape[2:])
    unseqs = lambda a: a.reshape((bsz, a.shape[0] // bsz) + a.shape[1:])
    dil_attn = functools.partial(_banded_attention, n_kv=DIL_HEADS, rep=1, max_off=BLOCK, want_lse=True)

    for l in range(depth):
        wl = w_in[l]
        cols_of = lambda off, width: wl[:, off:off + width]
        w_kd, w_vd = cols_of(IN_KD, B_KV), cols_of(IN_VD, B_KV)
        w_row = jnp.concatenate([cols_of(IN_U, SSM_WIDTH), w_kd, cols_of(IN_KA, A_KV)], axis=1).astype(BF16)
        w_col = jnp.concatenate([cols_of(IN_QA, A_Q), cols_of(IN_QD, B_KV), w_vd, cols_of(IN_VA, A_KV)],
                                axis=1).T.astype(BF16)
        w_mid_t = jnp.concatenate([cols_of(IN_QD + B_KV, B_KV), w_vd], axis=1).T.astype(BF16)
        w_wide_t = jnp.concatenate([cols_of(IN_QD + 2 * B_KV, B_KV), w_vd], axis=1).T.astype(BF16)
        w_gate = wl[:, W_MAIN:].astype(BF16)
        gain = row(norm_mix[l])
        rows, cols = _inproj(x, gain, w_row, w_col)
        k_mid, qv_mid, k_wide, qv_wide = _dil_proj(x, gain, w_kd.astype(BF16), w_mid_t, w_wide_t)

        ya = _banded_attention(cols, COL_QA // A_Q, rows, ROW_KA // A_KV, cols, COL_VA // A_KV,
                               n_kv=SWA_KV_HEADS, rep=SWA_Q_HEADS // SWA_KV_HEADS,
                               max_off=SWA_WINDOW - 1, sinks=attn_sinks[l])

        o0, l0 = dil_attn(cols, COL_QD // B_KV, rows, ROW_KD // B_KV, cols, COL_VD // B_KV)
        o1, l1 = dil_attn(seqs(qv_mid), 0, seqs(k_mid), 0, seqs(qv_mid), 1, perm_step=DIL_RATIO)
        o2, l2 = dil_attn(seqs(qv_wide), 0, seqs(k_wide), 0, seqs(qv_wide), 1)
        outs = [o0, unseqs(o1), unseqs(o2)]
        lses = [l0, unseqs(l1), unseqs(l2)]

        b_mat, c_mat = _ssm_matrices(bb_re[l], bb_im[l], ssm_c_re[l], ssm_c_im[l])
        bcast = lambda v: jnp.broadcast_to(v[None, :], (SUBLANES_V7X, N_STATE))
        yc = _ssm(rows, ROW_U // SSM_WIDTH, b_mat, bcast(a_re[l]), bcast(a_im[l]), c_mat, row(ssm_d[l]),
                  w_glu[l].astype(BF16), row(b_glu[l]))

        lane_w = LANES_V7X // DIL_HEADS
        w_b = w_branch_b[l].reshape(DIL_HEADS, HEAD_DIM // lane_w, lane_w, D_MODEL).transpose(1, 0, 2, 3)
        x = _merge(x, ya, outs, lses, yc, row(norm_mix[l]), w_gate, w_branch_a[l].astype(BF16),
                   w_b.reshape(B_KV, D_MODEL).astype(BF16), w_branch_c[l].astype(BF16), w_out[l].astype(BF16))
        x = _ffn(x, row(norm_ffn[l]), w_up[l].astype(BF16), conv_w[l], row(conv_b[l]),
                 w_down[l].astype(BF16), final_gain=row(norm_final) if l == depth - 1 else None)
    return x
```

```python
import functools
import math

import jax
import jax.numpy as jnp
from jax import lax
from jax.experimental import pallas as pl
from jax.experimental.pallas import tpu as pltpu

F32 = jnp.float32
BF16 = jnp.bfloat16

D_MODEL = 1024
HEAD_DIM = 64
BLOCK = 128
EPS = 1e-6
NEG_INF = -1e30
SWA_Q_HEADS = 8
SWA_KV_HEADS = 2
SWA_WINDOW = 128
DIL_PATTERNS = ((128, 1), (512, 4), (2048, 16))
N_DIL = 3
DIL_HEADS = 4
SSM_GROUP = 16
SSM_GROUPS = 32
SSM_WIDTH = SSM_GROUP * SSM_GROUPS
SSM_STATE = 64
N_STATE = SSM_GROUPS * SSM_STATE
N_BRANCH = 3
FFN_DIM = 2816
CONV_WIDTH = 3

A_Q = SWA_Q_HEADS * HEAD_DIM
A_KV = SWA_KV_HEADS * HEAD_DIM
B_Q = N_DIL * DIL_HEADS * HEAD_DIM
B_KV = DIL_HEADS * HEAD_DIM
GATE_W = N_BRANCH * D_MODEL
W_MAIN = A_Q + 2 * A_KV + B_Q + 2 * B_KV + SSM_WIDTH

IN_QA, IN_KA, IN_VA = 0, A_Q, A_Q + A_KV
IN_QD = A_Q + 2 * A_KV
IN_KD = IN_QD + B_Q
IN_VD = IN_KD + B_KV
IN_U = IN_VD + B_KV

ROW_W = SSM_WIDTH + B_KV + A_KV
ROW_U, ROW_KD, ROW_KA = 0, SSM_WIDTH, SSM_WIDTH + B_KV
COL_W = A_Q + B_KV + B_KV + A_KV
COL_QA, COL_QD, COL_VD, COL_VA = 0, A_Q, A_Q + B_KV, A_Q + 2 * B_KV
COL_Q_ROWS = A_Q + B_KV

LOG2_E = math.log2(math.e)
LN_2 = math.log(2.0)
Q_SCALE = HEAD_DIM ** -0.5 * LOG2_E

DIL_MID, DIL_WIDE = DIL_PATTERNS[1][1], DIL_PATTERNS[2][1]
DIL_RATIO = DIL_WIDE // DIL_MID
DIL_TILE = DIL_WIDE * BLOCK
PERM_CHUNK = BLOCK // DIL_RATIO

SUBLANES_V7X = 8
LANES_V7X = 128
MIB = 1024 * 1024

TOK_TILE = 512
PROJ_TILE = 1024
FFN_TILE = 512
ATTN_TILE = 1024
ATTN_LANES = 1024
SSM_T = 128
SSM_CHUNK = 256
FFN_CHUNK = 256


def _cparams(semantics, vmem_mib):
    return pltpu.CompilerParams(dimension_semantics=semantics, vmem_limit_bytes=vmem_mib * MIB)


def _resident(shape):
    nd = len(shape)
    return pl.BlockSpec(shape, lambda *_: (0,) * nd, pipeline_mode=pl.Buffered(1))


def _rmsnorm(x, g):
    return x * lax.rsqrt(jnp.mean(x * x, axis=-1, keepdims=True) + EPS) * g


def _dot(a, b):
    return jnp.dot(a, b, preferred_element_type=F32)


def _dot_nt(a, b):
    return lax.dot_general(a, b, (((1,), (1,)), ((), ())), preferred_element_type=F32)


def _inproj_body(x_ref, g_ref, wr_ref, wc_ref, row_ref, col_ref):
    h = _rmsnorm(x_ref[...], g_ref[...]).astype(BF16)
    row_ref[...] = _dot(h, wr_ref[...]).astype(BF16)
    col = _dot_nt(wc_ref[...], h)
    col_ref[:COL_Q_ROWS, :] = (col[:COL_Q_ROWS] * Q_SCALE).astype(BF16)
    col_ref[COL_Q_ROWS:, :] = col[COL_Q_ROWS:].astype(BF16)


def _inproj(x, gain, w_row, w_col):
    bsz, s, _ = x.shape
    return pl.pallas_call(
        _inproj_body,
        grid=(bsz, s // PROJ_TILE),
        in_specs=[pl.BlockSpec((None, PROJ_TILE, D_MODEL), lambda b, i: (b, i, 0)),
                  _resident((1, D_MODEL)), _resident(w_row.shape), _resident(w_col.shape)],
        out_specs=[pl.BlockSpec((None, PROJ_TILE, ROW_W), lambda b, i: (b, i, 0)),
                   pl.BlockSpec((None, COL_W, PROJ_TILE), lambda b, i: (b, 0, i))],
        out_shape=[jax.ShapeDtypeStruct((bsz, s, ROW_W), BF16),
                   jax.ShapeDtypeStruct((bsz, COL_W, s), BF16)],
        compiler_params=_cparams(("parallel", "parallel"), 40),
        name="inproj",
    )(x, gain, w_row, w_col)


def _dil_proj_body(x_ref, g_ref, wk_ref, wm_ref, ww_ref, km_ref, qvm_ref, kw_ref, qvw_ref,
                   hs_scr, sub_scr, hp_scr):
    n_slab = D_MODEL // LANES_V7X
    scale = Q_SCALE
    for c in range(DIL_TILE // TOK_TILE):
        rows = slice(c * TOK_TILE, (c + 1) * TOK_TILE)
        h = _rmsnorm(x_ref[rows, :], g_ref[...])
        for s in range(n_slab):
            hs_scr[s, rows, :] = h[:, s * LANES_V7X:(s + 1) * LANES_V7X]
    per_mid = DIL_TILE // DIL_MID
    for r4 in range(DIL_MID):
        for s in range(n_slab):
            sub_scr[s] = hs_scr[s, pl.ds(r4, per_mid, stride=DIL_MID), :]
        for m in range(DIL_RATIO):
            r = r4 + DIL_MID * m
            for s in range(n_slab):
                hp_scr[r * BLOCK:(r + 1) * BLOCK, s * LANES_V7X:(s + 1) * LANES_V7X] = (
                    sub_scr[s, pl.ds(m, BLOCK, stride=DIL_RATIO), :].astype(BF16))

    group = TOK_TILE // BLOCK
    for r0 in range(0, DIL_WIDE, group):
        hp = hp_scr[r0 * BLOCK:(r0 + group) * BLOCK, :]
        k = _dot(hp, wk_ref[...]).astype(BF16)
        qv = _dot_nt(ww_ref[...], hp)
        for n in range(group):
            cols = slice(n * BLOCK, (n + 1) * BLOCK)
            kw_ref[r0 + n] = k[cols, :]
            qvw_ref[r0 + n, :B_KV, :] = (qv[:B_KV, cols] * scale).astype(BF16)
            qvw_ref[r0 + n, B_KV:, :] = qv[B_KV:, cols].astype(BF16)
    per_res = DIL_TILE // DIL_MID
    for r4 in range(DIL_MID):
        hp = jnp.concatenate(
            [hp_scr[(r4 + DIL_MID * m) * BLOCK + PERM_CHUNK * j:(r4 + DIL_MID * m) * BLOCK + PERM_CHUNK * (j + 1), :]
             for j in range(per_res // BLOCK) for m in range(DIL_RATIO)], axis=0)
        km_ref[r4] = _dot(hp, wk_ref[...]).astype(BF16)
        qv = _dot_nt(wm_ref[...], hp)
        qvm_ref[r4, :B_KV, :] = (qv[:B_KV] * scale).astype(BF16)
        qvm_ref[r4, B_KV:, :] = qv[B_KV:].astype(BF16)


def _dil_proj(x, gain, w_k, w_mid_t, w_wide_t):
    bsz, s, _ = x.shape
    assert s % DIL_TILE == 0 and DIL_TILE % TOK_TILE == 0
    mid_len, wide_len = s // DIL_MID, s // DIL_WIDE
    per_res = DIL_TILE // DIL_MID
    return pl.pallas_call(
        _dil_proj_body,
        grid=(bsz, s // DIL_TILE),
        in_specs=[pl.BlockSpec((None, DIL_TILE, D_MODEL), lambda b, i: (b, i, 0)),
                  _resident(gain.shape), _resident(w_k.shape), _resident(w_mid_t.shape),
                  _resident(w_wide_t.shape)],
        out_specs=[pl.BlockSpec((None, DIL_MID, per_res, B_KV), lambda b, i: (b, 0, i, 0)),
                   pl.BlockSpec((None, DIL_MID, 2 * B_KV, per_res), lambda b, i: (b, 0, 0, i)),
                   pl.BlockSpec((None, DIL_WIDE, BLOCK, B_KV), lambda b, i: (b, 0, i, 0)),
                   pl.BlockSpec((None, DIL_WIDE, 2 * B_KV, BLOCK), lambda b, i: (b, 0, 0, i))],
        out_shape=[jax.ShapeDtypeStruct((bsz, DIL_MID, mid_len, B_KV), BF16),
                   jax.ShapeDtypeStruct((bsz, DIL_MID, 2 * B_KV, mid_len), BF16),
                   jax.ShapeDtypeStruct((bsz, DIL_WIDE, wide_len, B_KV), BF16),
                   jax.ShapeDtypeStruct((bsz, DIL_WIDE, 2 * B_KV, wide_len), BF16)],
        scratch_shapes=[pltpu.VMEM((D_MODEL // LANES_V7X, DIL_TILE, LANES_V7X), F32),
                        pltpu.VMEM((D_MODEL // LANES_V7X, DIL_TILE // DIL_MID, LANES_V7X), F32),
                        pltpu.VMEM((DIL_TILE, D_MODEL), BF16)],
        compiler_params=_cparams(("parallel", "parallel"), 56),
        name="dilated_proj",
    )(x, gain, w_k, w_mid_t, w_wide_t)


def _attn_body(*refs, n_kv, rep, max_off, tq, nseq, use_sink, want_lse, perm_step):
    qt_ref, kp_ref, k_ref, vtp_ref, vt_ref = refs[:5]
    pos = 5
    sink_ref = None
    if use_sink:
        sink_ref = refs[pos]
        pos += 1
    o_ref = refs[pos]
    lse_ref = refs[pos + 1] if want_lse else None

    i = pl.program_id(1)
    kpos = lax.broadcasted_iota(jnp.int32, (2 * BLOCK, BLOCK), 0)
    qidx = lax.broadcasted_iota(jnp.int32, (2 * BLOCK, BLOCK), 1)
    if perm_step:
        chunk = BLOCK // perm_step
        assert chunk & (chunk - 1) == 0 and BLOCK & (BLOCK - 1) == 0
        shift = chunk.bit_length() - 1
        place = lambda r: perm_step * (r & (chunk - 1)) + (r >> shift)
        kpos = (kpos & ~(BLOCK - 1)) + place(kpos & (BLOCK - 1))
        qidx = place(qidx)
    hi = qidx + BLOCK
    lo = hi - max_off
    band_cap = jnp.where(kpos <= hi, jnp.where(kpos >= lo, jnp.inf, NEG_INF), NEG_INF).astype(F32)
    first_lo = jnp.where(i > 0, 0, BLOCK)
    first_cap = jnp.where(kpos >= first_lo, band_cap, NEG_INF)
    zero_q = jnp.zeros((HEAD_DIM, rep * BLOCK), BF16)
    kv_unit = min(n_kv, max(1, ATTN_LANES // (rep * BLOCK)))
    n_heads = n_kv * rep
    lse_rows = LANES_V7X // n_heads

    blocks = [(sq, j) for sq in range(nseq) for j in range(tq // BLOCK)]
    units = []
    for sq, j in blocks:
        cap = first_cap if j == 0 else band_cap
        cap = jnp.concatenate([cap] * (kv_unit * rep), axis=1)
        cur = slice(j * BLOCK, (j + 1) * BLOCK)
        if j == 0:
            k_prev, vt_prev = kp_ref[sq], vtp_ref[sq]
        else:
            prv = slice((j - 1) * BLOCK, j * BLOCK)
            k_prev, vt_prev = k_ref[sq, prv, :], vt_ref[sq, :, prv]
        kk = jnp.concatenate([k_prev, k_ref[sq, cur, :]], axis=0)
        vvt = jnp.concatenate([vt_prev, vt_ref[sq, :, cur]], axis=1)
        for g0 in range(0, n_kv, kv_unit):
            unit = range(g0, g0 + kv_unit)
            heads = [g * rep + r for g in unit for r in range(rep)]
            qbd = jnp.concatenate(
                [jnp.concatenate(
                    [jnp.concatenate([qt_ref[sq, (g * rep + r) * HEAD_DIM:(g * rep + r + 1) * HEAD_DIM, cur]
                                      for r in range(rep)], axis=1) if g2 == g else zero_q
                     for g2 in range(n_kv)], axis=0)
                 for g in unit], axis=1)
            st = jnp.minimum(_dot(kk, qbd), cap)
            units.append(((sq, j), g0, heads, st, vvt[g0 * HEAD_DIM:(g0 + kv_unit) * HEAD_DIM, :]))

    soft = []
    for _, g0, heads, st, vt_u in units:
        m = jnp.max(st, axis=0, keepdims=True)
        sink_p = None
        if use_sink:
            sk = jnp.concatenate([jnp.full((1, BLOCK), sink_ref[h] * LOG2_E, F32) for h in heads], axis=1)
            m = jnp.maximum(m, sk)
            sink_p = jnp.exp2(sk - m)
        soft.append((jnp.exp2(st - m).astype(BF16), m, sink_p))

    out_t = {blk: [] for blk in blocks}
    lse_t = {blk: [] for blk in blocks}
    ones_rows = jnp.ones((SUBLANES_V7X, 2 * BLOCK), BF16)
    for (blk, g0, heads, st, vt_u), (p, m, sink_p) in zip(units, soft):
        pv = _dot(jnp.concatenate([vt_u, ones_rows], axis=0), p)
        l = pv[kv_unit * HEAD_DIM:kv_unit * HEAD_DIM + 1, :]
        if use_sink:
            l = l + sink_p
        ot_all = pv[:kv_unit * HEAD_DIM, :] * (1.0 / l)
        lse = (m + jnp.log2(l)) * LN_2 if want_lse else None
        for n, h in enumerate(heads):
            g = h // rep
            out_t[blk].append(ot_all[(g - g0) * HEAD_DIM:(g - g0 + 1) * HEAD_DIM, n * BLOCK:(n + 1) * BLOCK])
            if want_lse:
                lse_t[blk].append(jnp.broadcast_to(lse[:, n * BLOCK:(n + 1) * BLOCK], (lse_rows, BLOCK)))
    for sq, j in blocks:
        cur = slice(j * BLOCK, (j + 1) * BLOCK)
        heads_t = out_t[sq, j]
        assert len(heads_t) == n_heads
        if want_lse:
            heads_t = [ot[part * lse_rows:(part + 1) * lse_rows] for part in range(HEAD_DIM // lse_rows)
                       for ot in heads_t]
            lse_ref[sq, cur, :] = jnp.concatenate(lse_t[sq, j], axis=0).T
        o_ref[sq, cur, :] = jnp.concatenate(heads_t, axis=0).T.astype(o_ref.dtype)


def _banded_attention(qt_arr, q_blk, k_arr, k_blk, vt_arr, v_blk, *, n_kv, rep, max_off, sinks=None,
                      want_lse=False, perm_step=0):
    n, seq_len, _ = k_arr.shape
    tq = min(ATTN_TILE, seq_len)
    nseq = ATTN_TILE // tq
    assert seq_len % tq == 0 and n % nseq == 0
    per = tq // BLOCK
    qw = n_kv * rep * HEAD_DIM
    kw = n_kv * HEAD_DIM
    prev_blk = lambda i: jnp.maximum(i * per - 1, 0)
    in_specs = [pl.BlockSpec((nseq, qw, tq), lambda b, i: (b, q_blk, i)),
                pl.BlockSpec((nseq, BLOCK, kw), lambda b, i: (b, prev_blk(i), k_blk)),
                pl.BlockSpec((nseq, tq, kw), lambda b, i: (b, i, k_blk)),
                pl.BlockSpec((nseq, kw, BLOCK), lambda b, i: (b, v_blk, prev_blk(i))),
                pl.BlockSpec((nseq, kw, tq), lambda b, i: (b, v_blk, i))]
    args = [qt_arr, k_arr, k_arr, vt_arr, vt_arr]
    if sinks is not None:
        in_specs.append(pl.BlockSpec(memory_space=pltpu.SMEM))
        args.append(sinks)
    out_spec = pl.BlockSpec((nseq, tq, qw), lambda b, i: (b, i, 0))
    out_shape = [jax.ShapeDtypeStruct((n, seq_len, qw), BF16)]
    out_specs = [out_spec]
    if want_lse:
        assert LANES_V7X % (n_kv * rep) == 0 and HEAD_DIM % (LANES_V7X // (n_kv * rep)) == 0
        out_shape.append(jax.ShapeDtypeStruct((n, seq_len, LANES_V7X), F32))
        out_specs.append(pl.BlockSpec((nseq, tq, LANES_V7X), lambda b, i: (b, i, 0)))
    body = functools.partial(_attn_body, n_kv=n_kv, rep=rep, max_off=max_off, tq=tq, nseq=nseq,
                             use_sink=sinks is not None, want_lse=want_lse, perm_step=perm_step)
    res = pl.pallas_call(
        body,
        grid=(n // nseq, seq_len // tq),
        in_specs=in_specs,
        out_specs=out_specs,
        out_shape=out_shape,
        compiler_params=_cparams(("parallel", "parallel"), 48),
        name="banded_attention",
    )(*args)
    return res if want_lse else res[0]


def _ssm_param_body(lr_ref, li_ref, ldt_ref, br_ref, bi_ref, ar_ref, ai_ref, bbr_ref, bbi_ref):
    lr, li = lr_ref[...], li_ref[...]
    dt = jnp.exp(ldt_ref[...])
    mag = jnp.exp(lr * dt)
    ab_re, ab_im = mag * jnp.cos(li * dt), mag * jnp.sin(li * dt)
    nr, ni = ab_re - 1.0, ab_im
    den = lr * lr + li * li
    f_re = (nr * lr + ni * li) / den
    f_im = (ni * lr - nr * li) / den
    br, bi = br_ref[...], bi_ref[...]
    ar_ref[...] = ab_re
    ai_ref[...] = ab_im
    bbr_ref[...] = f_re * br - f_im * bi
    bbi_ref[...] = f_re * bi + f_im * br


def _ssm_params(lam_re, lam_im, log_dt, b_re, b_im):
    nl = lam_re.shape[0]
    rows = nl * SSM_GROUPS * SSM_GROUP
    rep = lambda a: jnp.broadcast_to(a[:, :, None, :], (nl, SSM_GROUPS, SSM_GROUP, SSM_STATE)).reshape(
        rows, SSM_STATE)
    ldt = jnp.broadcast_to(log_dt[:, :, None, None], (nl, SSM_GROUPS, SSM_GROUP, SSM_STATE)).reshape(
        rows, SSM_STATE)
    bt = lambda b: jnp.swapaxes(b, 2, 3).reshape(rows, SSM_STATE)
    shape = jax.ShapeDtypeStruct((rows, SSM_STATE), F32)
    ar, ai, bbr, bbi = pl.pallas_call(
        _ssm_param_body, out_shape=[shape] * 4, name="ssm_params",
    )(rep(lam_re), rep(lam_im), ldt, bt(b_re), bt(b_im))
    pick = lambda a: a.reshape(nl, SSM_GROUPS, SSM_GROUP, SSM_STATE)[:, :, 0, :].reshape(nl, N_STATE)
    unflat = lambda a: a.reshape(nl, SSM_GROUPS, SSM_GROUP, SSM_STATE)
    return pick(ar), pick(ai), unflat(bbr), unflat(bbi)


SSM_HALF_GROUPS = SSM_GROUPS // 2
SSM_HALF_IN = SSM_HALF_GROUPS * SSM_GROUP
SSM_HALF_STATE = SSM_HALF_GROUPS * SSM_STATE


def _ssm_matrices(bb_re, bb_im, c_re, c_im):
    eye = jnp.eye(SSM_HALF_GROUPS, dtype=F32)

    def in_blocks(bb):
        bb = bb.reshape(2, SSM_HALF_GROUPS, SSM_GROUP, SSM_STATE)
        return jnp.einsum("ab,kahp->kahbp", eye, bb).reshape(2, SSM_HALF_IN, SSM_HALF_STATE)

    def out_blocks(c):
        c = c.reshape(2, SSM_HALF_GROUPS, SSM_GROUP, SSM_STATE)
        return jnp.einsum("ab,kahp->kapbh", eye, c).reshape(2, SSM_HALF_STATE, SSM_HALF_IN)

    b_mat = jnp.concatenate([in_blocks(bb_re), in_blocks(bb_im)], axis=2).astype(BF16)
    c_mat = jnp.concatenate([out_blocks(c_re), out_blocks(-c_im)], axis=1).astype(BF16)
    return b_mat, c_mat


def _ssm_body(u_ref, bm_ref, ar_ref, ai_ref, cm_ref, d_ref, wg_ref, bg_ref, o_ref,
              x_scr, sr_scr, si_scr, tb_scr):
    nb = SUBLANES_V7X
    width = 2 * SSM_HALF_STATE
    n_slab = SSM_WIDTH // LANES_V7X

    @pl.when(pl.program_id(0) == 0)
    def _():
        sr_scr[...] = jnp.zeros_like(sr_scr)
        si_scr[...] = jnp.zeros_like(si_scr)

    for b in range(nb):
        ub = u_ref[b].astype(F32)
        for s in range(n_slab):
            tb_scr[s, pl.ds(b, SSM_T, stride=nb), :] = ub[:, s * LANES_V7X:(s + 1) * LANES_V7X]
    u32 = jnp.concatenate([tb_scr[s] for s in range(n_slab)], axis=1)
    u = u32.astype(BF16)

    per_half = SSM_HALF_STATE // SSM_CHUNK
    chunks = [(half, c) for half in range(2) for c in range(per_half)]

    def cols_of(half, c):
        re0 = half * width + c * SSM_CHUNK
        return re0, re0 + SSM_HALF_STATE, half * SSM_HALF_STATE + c * SSM_CHUNK

    def input_map(half, c):
        re0, im0, _ = cols_of(half, c)
        uh = u[:, half * SSM_HALF_IN:(half + 1) * SSM_HALF_IN]
        for c0, b0 in ((re0, c * SSM_CHUNK), (im0, SSM_HALF_STATE + c * SSM_CHUNK)):
            x_scr[:, c0:c0 + SSM_CHUNK] = _dot(uh, bm_ref[half, :, b0:b0 + SSM_CHUNK])

    def recurrence(half, c):
        re0, im0, st0 = cols_of(half, c)
        a_re = ar_ref[:, st0:st0 + SSM_CHUNK]
        a_im = ai_ref[:, st0:st0 + SSM_CHUNK]
        xr, xi = sr_scr[:, st0:st0 + SSM_CHUNK], si_scr[:, st0:st0 + SSM_CHUNK]
        for t in range(SSM_T):
            rows = slice(t * nb, (t + 1) * nb)
            xr, xi = (a_re * xr - a_im * xi + x_scr[rows, re0:re0 + SSM_CHUNK],
                      a_re * xi + a_im * xr + x_scr[rows, im0:im0 + SSM_CHUNK])
            x_scr[rows, re0:re0 + SSM_CHUNK] = xr
            x_scr[rows, im0:im0 + SSM_CHUNK] = xi
        sr_scr[:, st0:st0 + SSM_CHUNK] = xr
        si_scr[:, st0:st0 + SSM_CHUNK] = xi

    def output_map(half, c):
        re0, im0, _ = cols_of(half, c)
        acc = None
        for c0, r0 in ((re0, c * SSM_CHUNK), (im0, SSM_HALF_STATE + c * SSM_CHUNK)):
            part = _dot(x_scr[:, c0:c0 + SSM_CHUNK].astype(BF16), cm_ref[half, r0:r0 + SSM_CHUNK, :])
            acc = part if acc is None else acc + part
        return acc

    ys = [None, None]
    n = len(chunks)
    for stage in range(n + 2):
        if stage < n:
            input_map(*chunks[stage])
        if 1 <= stage <= n:
            recurrence(*chunks[stage - 1])
        if stage >= 2:
            half, c = chunks[stage - 2]
            part = output_map(half, c)
            ys[half] = part if ys[half] is None else ys[half] + part
    y = jnp.concatenate(ys, axis=1) + d_ref[...] * u32
    z = jax.nn.gelu(y)
    z = z * jax.nn.sigmoid(_dot(z.astype(BF16), wg_ref[...]) + bg_ref[...])
    for s in range(n_slab):
        tb_scr[s] = z[:, s * LANES_V7X:(s + 1) * LANES_V7X]
    for b in range(nb):
        for s in range(n_slab):
            o_ref[b, :, s * LANES_V7X:(s + 1) * LANES_V7X] = (
                tb_scr[s, pl.ds(b, SSM_T, stride=nb), :].astype(o_ref.dtype))


def _ssm(u_arr, u_blk, b_mat, a_re, a_im, c_mat, d_skip, w_glu, b_glu):
    bsz, s, _ = u_arr.shape
    rows = SSM_T * SUBLANES_V7X
    return pl.pallas_call(
        _ssm_body,
        grid=(s // SSM_T,),
        in_specs=[pl.BlockSpec((bsz, SSM_T, SSM_WIDTH), lambda i: (0, i, u_blk)),
                  _resident(b_mat.shape), _resident(a_re.shape), _resident(a_im.shape),
                  _resident(c_mat.shape), _resident(d_skip.shape), _resident(w_glu.shape),
                  _resident(b_glu.shape)],
        out_specs=pl.BlockSpec((bsz, SSM_T, SSM_WIDTH), lambda i: (0, i, 0)),
        out_shape=jax.ShapeDtypeStruct((bsz, s, SSM_WIDTH), BF16),
        scratch_shapes=[pltpu.VMEM((rows, 2 * N_STATE), F32),
                        pltpu.VMEM((SUBLANES_V7X, N_STATE), F32),
                        pltpu.VMEM((SUBLANES_V7X, N_STATE), F32),
                        pltpu.VMEM((SSM_WIDTH // LANES_V7X, rows, LANES_V7X), F32)],
        compiler_params=_cparams(("arbitrary",), 40),
        name="s5_mixer",
    )(u_arr, b_mat, a_re, a_im, c_mat, d_skip, w_glu, b_glu)


def _merge_body(x_ref, ya_ref, o0_ref, o1_ref, o2_ref, l0_ref, l1_ref, l2_ref, yc_ref,
                gn_ref, wg_ref, wa_ref, wb_ref, wc_ref, wo_ref, out_ref, merged_scr, sub_scr, tok_scr):
    x = x_ref[...]
    h = _rmsnorm(x, gn_ref[...]).astype(BF16)
    n_slab = B_KV // LANES_V7X

    def token_order(slot, piece_of, width):
        for s in range(width // LANES_V7X):
            lanes = slice(s * LANES_V7X, (s + 1) * LANES_V7X)
            for r in range(DIL_WIDE):
                r4, m = r % DIL_MID, r // DIL_MID
                sub_scr[slot, s, r4, pl.ds(m, PERM_CHUNK, stride=DIL_RATIO), :] = piece_of(r)[:, lanes].astype(F32)
            for r4 in range(DIL_MID):
                tok_scr[slot, s, pl.ds(r4, BLOCK, stride=DIL_MID), :] = sub_scr[slot, s, r4]
        return [tok_scr[slot, s] for s in range(width // LANES_V7X)]

    mid = lambda ref: (lambda r: ref[r % DIL_MID, (r // DIL_MID) * PERM_CHUNK:(r // DIL_MID + 1) * PERM_CHUNK, :])
    wide = lambda ref: (lambda r: ref[r])
    o1, (l1,) = token_order(0, mid(o1_ref), B_KV), token_order(1, mid(l1_ref), LANES_V7X)
    o2, (l2,) = token_order(2, wide(o2_ref), B_KV), token_order(3, wide(l2_ref), LANES_V7X)
    l0 = l0_ref[...]
    top = jnp.maximum(jnp.maximum(l0, l1), l2)
    e0, e1, e2 = jnp.exp(l0 - top), jnp.exp(l1 - top), jnp.exp(l2 - top)
    inv = 1.0 / (e0 + e1 + e2)
    w0, w1, w2 = e0 * inv, e1 * inv, e2 * inv
    yb = jnp.concatenate(
        [w0 * o0_ref[:, s * LANES_V7X:(s + 1) * LANES_V7X].astype(F32) + w1 * o1[s] + w2 * o2[s]
         for s in range(n_slab)], axis=1).astype(BF16)
    ya = ya_ref[...]
    yc = yc_ref[...]
    step = 256
    for c in range(D_MODEL // step):
        cs = slice(c * step, (c + 1) * step)
        gate = lambda k: jax.nn.sigmoid(_dot(h, wg_ref[:, k * D_MODEL + c * step:k * D_MODEL + (c + 1) * step]))
        merged = (gate(0) * _dot(ya, wa_ref[:, cs]) + gate(1) * _dot(yb, wb_ref[:, cs])
                  + gate(2) * _dot(yc, wc_ref[:, cs]))
        merged_scr[:, cs] = merged.astype(BF16)
    out_ref[...] = x + _dot(merged_scr[...], wo_ref[...])


def _merge(x, ya, outs, lses, yc, gain, w_gate, w_a, w_b, w_c, w_out):
    bsz, s, _ = x.shape
    assert TOK_TILE == DIL_WIDE * PERM_CHUNK and TOK_TILE // DIL_MID == BLOCK
    tile = lambda width: pl.BlockSpec((None, TOK_TILE, width), lambda b, i: (b, i, 0))
    mid = lambda width: pl.BlockSpec((None, DIL_MID, BLOCK, width), lambda b, i: (b, 0, i, 0))
    wide = lambda width: pl.BlockSpec((None, DIL_WIDE, PERM_CHUNK, width), lambda b, i: (b, 0, i, 0))
    groups = lambda width: [tile(width), mid(width), wide(width)]
    n_slab = B_KV // LANES_V7X
    return pl.pallas_call(
        _merge_body,
        grid=(bsz, s // TOK_TILE),
        in_specs=[tile(D_MODEL), tile(A_Q)] + groups(B_KV) + groups(LANES_V7X) + [tile(SSM_WIDTH),
                  _resident(gain.shape), _resident(w_gate.shape), _resident(w_a.shape),
                  _resident(w_b.shape), _resident(w_c.shape), _resident(w_out.shape)],
        out_specs=tile(D_MODEL),
        out_shape=jax.ShapeDtypeStruct(x.shape, x.dtype),
        scratch_shapes=[pltpu.VMEM((TOK_TILE, D_MODEL), BF16),
                        pltpu.VMEM((4, n_slab, DIL_MID, BLOCK, LANES_V7X), F32),
                        pltpu.VMEM((4, n_slab, TOK_TILE, LANES_V7X), F32)],
        compiler_params=_cparams(("parallel", "parallel"), 48),
        name="gated_merge",
    )(x, ya, *outs, *lses, yc, gain, w_gate, w_a, w_b, w_c, w_out)


def _ffn_body(*refs, final):
    x_ref, gn_ref, wup_ref, cw_ref, cb_ref, wdn_ref = refs[:6]
    pos = 6
    gf_ref = None
    if final:
        gf_ref = refs[pos]
        pos += 1
    out_ref, carry_scr, work_scr, act_scr = refs[pos:pos + 4]
    pad = SUBLANES_V7X

    @pl.when(pl.program_id(1) == 0)
    def _():
        carry_scr[...] = jnp.zeros_like(carry_scr)

    x = x_ref[...]
    h = _rmsnorm(x, gn_ref[...]).astype(BF16)

    def conv_cols(slot, c0):
        cs = slice(c0, c0 + FFN_CHUNK)
        up = _dot(h, wup_ref[:, cs])
        work_scr[slot, 0:pad, :] = carry_scr[:, cs]
        work_scr[slot, pad:pad + FFN_TILE, :] = up
        carry_scr[:, cs] = up[FFN_TILE - pad:, :]
        acc = cb_ref[:, cs] + cw_ref[CONV_WIDTH - 1:CONV_WIDTH, cs] * up
        for k in range(CONV_WIDTH - 1):
            shift = CONV_WIDTH - 1 - k
            acc = acc + cw_ref[k:k + 1, cs] * work_scr[slot, pad - shift:pad - shift + FFN_TILE, :]
        return acc

    for c in range(FFN_DIM // FFN_CHUNK):
        gate = conv_cols(0, c * FFN_CHUNK)
        val = conv_cols(1, FFN_DIM + c * FFN_CHUNK)
        act_scr[:, c * FFN_CHUNK:(c + 1) * FFN_CHUNK] = (jax.nn.silu(gate) * val).astype(BF16)
    y = x + _dot(act_scr[...], wdn_ref[...])
    if final:
        y = _rmsnorm(y, gf_ref[...])
    out_ref[...] = y


def _ffn(x, gain, w_up, conv_w, conv_b, w_down, final_gain=None):
    bsz, s, _ = x.shape
    final = final_gain is not None
    tile = pl.BlockSpec((None, FFN_TILE, D_MODEL), lambda b, i: (b, i, 0))
    in_specs = [tile, _resident(gain.shape), _resident(w_up.shape), _resident(conv_w.shape),
                _resident(conv_b.shape), _resident(w_down.shape)]
    args = [x, gain, w_up, conv_w, conv_b, w_down]
    if final:
        in_specs.append(_resident(final_gain.shape))
        args.append(final_gain)
    return pl.pallas_call(
        functools.partial(_ffn_body, final=final),
        grid=(bsz, s // FFN_TILE),
        in_specs=in_specs,
        out_specs=tile,
        out_shape=jax.ShapeDtypeStruct(x.shape, x.dtype),
        scratch_shapes=[pltpu.VMEM((SUBLANES_V7X, 2 * FFN_DIM), F32),
                        pltpu.VMEM((2, SUBLANES_V7X + FFN_TILE, FFN_CHUNK), F32),
                        pltpu.VMEM((FFN_TILE, FFN_DIM), BF16)],
        compiler_params=_cparams(("arbitrary", "arbitrary"), 56),
        name="conv_ffn",
    )(*args)


def kernel(x, norm_mix, w_in, attn_sinks, ssm_lambda_re, ssm_lambda_im, ssm_log_dt, ssm_b_re, ssm_b_im,
           ssm_c_re, ssm_c_im, ssm_d, w_glu, b_glu, w_branch_a, w_branch_b, w_branch_c, w_out,
           norm_ffn, w_up, conv_w, conv_b, w_down, norm_final):
    bsz, s, _ = x.shape
    depth = w_in.shape[0]
    assert bsz == SUBLANES_V7X, "the S5 kernel maps the batch onto the sublane axis"
    assert s % TOK_TILE == 0 and s % PROJ_TILE == 0 and s % FFN_TILE == 0 and s % SSM_T == 0
    assert [d for _, d in DIL_PATTERNS] == [1, DIL_MID, DIL_WIDE]
    assert all(w // d == BLOCK for w, d in DIL_PATTERNS), "every dilated group spans one BLOCK of its sub-sequence"

    a_re, a_im, bb_re, bb_im = _ssm_params(ssm_lambda_re, ssm_lambda_im, ssm_log_dt, ssm_b_re, ssm_b_im)
    row = lambda v: v.reshape(1, -1)
    seqs = lambda a: a.reshape((a.shape[0] * a.shape[1],) + a.shape[2:])
    unseqs = lambda a: a.reshape((bsz, a.shape[0] // bsz) + a.shape[1:])
    dil_attn = functools.partial(_banded_attention, n_kv=DIL_HEADS, rep=1, max_off=BLOCK, want_lse=True)

    for l in range(depth):
        wl = w_in[l]
        cols_of = lambda off, width: wl[:, off:off + width]
        w_kd, w_vd = cols_of(IN_KD, B_KV), cols_of(IN_VD, B_KV)
        w_row = jnp.concatenate([cols_of(IN_U, SSM_WIDTH), w_kd, cols_of(IN_KA, A_KV)], axis=1).astype(BF16)
        w_col = jnp.concatenate([cols_of(IN_QA, A_Q), cols_of(IN_QD, B_KV), w_vd, cols_of(IN_VA, A_KV)],
                                axis=1).T.astype(BF16)
        w_mid_t = jnp.concatenate([cols_of(IN_QD + B_KV, B_KV), w_vd], axis=1).T.astype(BF16)
        w_wide_t = jnp.concatenate([cols_of(IN_QD + 2 * B_KV, B_KV), w_vd], axis=1).T.astype(BF16)
        w_gate = wl[:, W_MAIN:].astype(BF16)
        gain = row(norm_mix[l])
        rows, cols = _inproj(x, gain, w_row, w_col)
        k_mid, qv_mid, k_wide, qv_wide = _dil_proj(x, gain, w_kd.astype(BF16), w_mid_t, w_wide_t)

        ya = _banded_attention(cols, COL_QA // A_Q, rows, ROW_KA // A_KV, cols, COL_VA // A_KV,
                               n_kv=SWA_KV_HEADS, rep=SWA_Q_HEADS // SWA_KV_HEADS,
                               max_off=SWA_WINDOW - 1, sinks=attn_sinks[l])

        o0, l0 = dil_attn(cols, COL_QD // B_KV, rows, ROW_KD // B_KV, cols, COL_VD // B_KV)
        o1, l1 = dil_attn(seqs(qv_mid), 0, seqs(k_mid), 0, seqs(qv_mid), 1, perm_step=DIL_RATIO)
        o2, l2 = dil_attn(seqs(qv_wide), 0, seqs(k_wide), 0, seqs(qv_wide), 1)
        outs = [o0, unseqs(o1), unseqs(o2)]
        lses = [l0, unseqs(l1), unseqs(l2)]

        b_mat, c_mat = _ssm_matrices(bb_re[l], bb_im[l], ssm_c_re[l], ssm_c_im[l])
        bcast = lambda v: jnp.broadcast_to(v[None, :], (SUBLANES_V7X, N_STATE))
        yc = _ssm(rows, ROW_U // SSM_WIDTH, b_mat, bcast(a_re[l]), bcast(a_im[l]), c_mat, row(ssm_d[l]),
                  w_glu[l].astype(BF16), row(b_glu[l]))

        lane_w = LANES_V7X // DIL_HEADS
        w_b = w_branch_b[l].reshape(DIL_HEADS, HEAD_DIM // lane_w, lane_w, D_MODEL).transpose(1, 0, 2, 3)
        x = _merge(x, ya, outs, lses, yc, row(norm_mix[l]), w_gate, w_branch_a[l].astype(BF16),
                   w_b.reshape(B_KV, D_MODEL).astype(BF16), w_branch_c[l].astype(BF16), w_out[l].astype(BF16))
        x = _ffn(x, row(norm_ffn[l]), w_up[l].astype(BF16), conv_w[l], row(conv_b[l]),
                 w_down[l].astype(BF16), final_gain=row(norm_final) if l == depth - 1 else None)
    return x
```

```python
import functools
import math

import jax
import jax.numpy as jnp
from jax import lax
from jax.experimental import pallas as pl
from jax.experimental.pallas import tpu as pltpu

F32 = jnp.float32
BF16 = jnp.bfloat16

D_MODEL = 1024
HEAD_DIM = 64
BLOCK = 128
EPS = 1e-6
NEG_INF = -1e30
SWA_Q_HEADS = 8
SWA_KV_HEADS = 2
SWA_WINDOW = 128
DIL_PATTERNS = ((128, 1), (512, 4), (2048, 16))
N_DIL = 3
DIL_HEADS = 4
SSM_GROUP = 16
SSM_GROUPS = 32
SSM_WIDTH = SSM_GROUP * SSM_GROUPS
SSM_STATE = 64
N_STATE = SSM_GROUPS * SSM_STATE
N_BRANCH = 3
FFN_DIM = 2816
CONV_WIDTH = 3

A_Q = SWA_Q_HEADS * HEAD_DIM
A_KV = SWA_KV_HEADS * HEAD_DIM
B_Q = N_DIL * DIL_HEADS * HEAD_DIM
B_KV = DIL_HEADS * HEAD_DIM
GATE_W = N_BRANCH * D_MODEL
W_MAIN = A_Q + 2 * A_KV + B_Q + 2 * B_KV + SSM_WIDTH

IN_QA, IN_KA, IN_VA = 0, A_Q, A_Q + A_KV
IN_QD = A_Q + 2 * A_KV
IN_KD = IN_QD + B_Q
IN_VD = IN_KD + B_KV
IN_U = IN_VD + B_KV

ROW_W = SSM_WIDTH + B_KV + A_KV
ROW_U, ROW_KD, ROW_KA = 0, SSM_WIDTH, SSM_WIDTH + B_KV
COL_W = A_Q + B_KV + B_KV + A_KV
COL_QA, COL_QD, COL_VD, COL_VA = 0, A_Q, A_Q + B_KV, A_Q + 2 * B_KV
COL_Q_ROWS = A_Q + B_KV

LOG2_E = math.log2(math.e)
LN_2 = math.log(2.0)
Q_SCALE = HEAD_DIM ** -0.5 * LOG2_E

DIL_MID, DIL_WIDE = DIL_PATTERNS[1][1], DIL_PATTERNS[2][1]
DIL_RATIO = DIL_WIDE // DIL_MID
DIL_TILE = DIL_WIDE * BLOCK
PERM_CHUNK = BLOCK // DIL_RATIO

SUBLANES_V7X = 8
LANES_V7X = 128
MIB = 1024 * 1024

TOK_TILE = 512
PROJ_TILE = 1024
FFN_TILE = 512
ATTN_TILE = 2048
ATTN_LANES = 1024
SSM_T = 128
SSM_CHUNK = 256
FFN_CHUNK = 256


def _cparams(semantics, vmem_mib):
    return pltpu.CompilerParams(dimension_semantics=semantics, vmem_limit_bytes=vmem_mib * MIB)


def _resident(shape):
    nd = len(shape)
    return pl.BlockSpec(shape, lambda *_: (0,) * nd, pipeline_mode=pl.Buffered(1))


def _rmsnorm(x, g):
    return x * lax.rsqrt(jnp.mean(x * x, axis=-1, keepdims=True) + EPS) * g


def _dot(a, b):
    return jnp.dot(a, b, preferred_element_type=F32)


def _dot_nt(a, b):
    return lax.dot_general(a, b, (((1,), (1,)), ((), ())), preferred_element_type=F32)


def _inproj_body(x_ref, g_ref, wr_ref, wc_ref, row_ref, col_ref):
    h = _rmsnorm(x_ref[...], g_ref[...]).astype(BF16)
    row_ref[...] = _dot(h, wr_ref[...]).astype(BF16)
    col = _dot_nt(wc_ref[...], h)
    col_ref[:COL_Q_ROWS, :] = (col[:COL_Q_ROWS] * Q_SCALE).astype(BF16)
    col_ref[COL_Q_ROWS:, :] = col[COL_Q_ROWS:].astype(BF16)


def _inproj(x, gain, w_row, w_col):
    bsz, s, _ = x.shape
    return pl.pallas_call(
        _inproj_body,
        grid=(bsz, s // PROJ_TILE),
        in_specs=[pl.BlockSpec((None, PROJ_TILE, D_MODEL), lambda b, i: (b, i, 0)),
                  _resident((1, D_MODEL)), _resident(w_row.shape), _resident(w_col.shape)],
        out_specs=[pl.BlockSpec((None, PROJ_TILE, ROW_W), lambda b, i: (b, i, 0)),
                   pl.BlockSpec((None, COL_W, PROJ_TILE), lambda b, i: (b, 0, i))],
        out_shape=[jax.ShapeDtypeStruct((bsz, s, ROW_W), BF16),
                   jax.ShapeDtypeStruct((bsz, COL_W, s), BF16)],
        compiler_params=_cparams(("parallel", "parallel"), 40),
        name="inproj",
    )(x, gain, w_row, w_col)


def _dil_proj_body(x_ref, g_ref, wk_ref, wm_ref, ww_ref, km_ref, qvm_ref, kw_ref, qvw_ref,
                   hs_scr, sub_scr, hp_scr):
    n_slab = D_MODEL // LANES_V7X
    scale = Q_SCALE
    for c in range(DIL_TILE // TOK_TILE):
        rows = slice(c * TOK_TILE, (c + 1) * TOK_TILE)
        h = _rmsnorm(x_ref[rows, :], g_ref[...])
        for s in range(n_slab):
            hs_scr[s, rows, :] = h[:, s * LANES_V7X:(s + 1) * LANES_V7X]
    per_mid = DIL_TILE // DIL_MID
    for r4 in range(DIL_MID):
        for s in range(n_slab):
            sub_scr[s] = hs_scr[s, pl.ds(r4, per_mid, stride=DIL_MID), :]
        for m in range(DIL_RATIO):
            r = r4 + DIL_MID * m
            for s in range(n_slab):
                hp_scr[r * BLOCK:(r + 1) * BLOCK, s * LANES_V7X:(s + 1) * LANES_V7X] = (
                    sub_scr[s, pl.ds(m, BLOCK, stride=DIL_RATIO), :].astype(BF16))

    group = TOK_TILE // BLOCK
    for r0 in range(0, DIL_WIDE, group):
        hp = hp_scr[r0 * BLOCK:(r0 + group) * BLOCK, :]
        k = _dot(hp, wk_ref[...]).astype(BF16)
        qv = _dot_nt(ww_ref[...], hp)
        for n in range(group):
            cols = slice(n * BLOCK, (n + 1) * BLOCK)
            kw_ref[r0 + n] = k[cols, :]
            qvw_ref[r0 + n, :B_KV, :] = (qv[:B_KV, cols] * scale).astype(BF16)
            qvw_ref[r0 + n, B_KV:, :] = qv[B_KV:, cols].astype(BF16)
    per_res = DIL_TILE // DIL_MID
    for r4 in range(DIL_MID):
        hp = jnp.concatenate(
            [hp_scr[(r4 + DIL_MID * m) * BLOCK + PERM_CHUNK * j:(r4 + DIL_MID * m) * BLOCK + PERM_CHUNK * (j + 1), :]
             for j in range(per_res // BLOCK) for m in range(DIL_RATIO)], axis=0)
        km_ref[r4] = _dot(hp, wk_ref[...]).astype(BF16)
        qv = _dot_nt(wm_ref[...], hp)
        qvm_ref[r4, :B_KV, :] = (qv[:B_KV] * scale).astype(BF16)
        qvm_ref[r4, B_KV:, :] = qv[B_KV:].astype(BF16)


def _dil_proj(x, gain, w_k, w_mid_t, w_wide_t):
    bsz, s, _ = x.shape
    assert s % DIL_TILE == 0 and DIL_TILE % TOK_TILE == 0
    mid_len, wide_len = s // DIL_MID, s // DIL_WIDE
    per_res = DIL_TILE // DIL_MID
    return pl.pallas_call(
        _dil_proj_body,
        grid=(bsz, s // DIL_TILE),
        in_specs=[pl.BlockSpec((None, DIL_TILE, D_MODEL), lambda b, i: (b, i, 0)),
                  _resident(gain.shape), _resident(w_k.shape), _resident(w_mid_t.shape),
                  _resident(w_wide_t.shape)],
        out_specs=[pl.BlockSpec((None, DIL_MID, per_res, B_KV), lambda b, i: (b, 0, i, 0)),
                   pl.BlockSpec((None, DIL_MID, 2 * B_KV, per_res), lambda b, i: (b, 0, 0, i)),
                   pl.BlockSpec((None, DIL_WIDE, BLOCK, B_KV), lambda b, i: (b, 0, i, 0)),
                   pl.BlockSpec((None, DIL_WIDE, 2 * B_KV, BLOCK), lambda b, i: (b, 0, 0, i))],
        out_shape=[jax.ShapeDtypeStruct((bsz, DIL_MID, mid_len, B_KV), BF16),
                   jax.ShapeDtypeStruct((bsz, DIL_MID, 2 * B_KV, mid_len), BF16),
                   jax.ShapeDtypeStruct((bsz, DIL_WIDE, wide_len, B_KV), BF16),
                   jax.ShapeDtypeStruct((bsz, DIL_WIDE, 2 * B_KV, wide_len), BF16)],
        scratch_shapes=[pltpu.VMEM((D_MODEL // LANES_V7X, DIL_TILE, LANES_V7X), F32),
                        pltpu.VMEM((D_MODEL // LANES_V7X, DIL_TILE // DIL_MID, LANES_V7X), F32),
                        pltpu.VMEM((DIL_TILE, D_MODEL), BF16)],
        compiler_params=_cparams(("parallel", "parallel"), 56),
        name="dilated_proj",
    )(x, gain, w_k, w_mid_t, w_wide_t)


def _attn_body(*refs, n_kv, rep, max_off, tq, nseq, use_sink, want_lse, perm_step):
    qt_ref, kp_ref, k_ref, vtp_ref, vt_ref = refs[:5]
    pos = 5
    sink_ref = None
    if use_sink:
        sink_ref = refs[pos]
        pos += 1
    o_ref = refs[pos]
    lse_ref = refs[pos + 1] if want_lse else None

    i = pl.program_id(1)
    kpos = lax.broadcasted_iota(jnp.int32, (2 * BLOCK, BLOCK), 0)
    qidx = lax.broadcasted_iota(jnp.int32, (2 * BLOCK, BLOCK), 1)
    if perm_step:
        chunk = BLOCK // perm_step
        assert chunk & (chunk - 1) == 0 and BLOCK & (BLOCK - 1) == 0
        shift = chunk.bit_length() - 1
        place = lambda r: perm_step * (r & (chunk - 1)) + (r >> shift)
        kpos = (kpos & ~(BLOCK - 1)) + place(kpos & (BLOCK - 1))
        qidx = place(qidx)
    hi = qidx + BLOCK
    lo = hi - max_off
    band_cap = jnp.where(kpos <= hi, jnp.where(kpos >= lo, jnp.inf, NEG_INF), NEG_INF).astype(F32)
    first_lo = jnp.where(i > 0, 0, BLOCK)
    first_cap = jnp.where(kpos >= first_lo, band_cap, NEG_INF)
    zero_q = jnp.zeros((HEAD_DIM, rep * BLOCK), BF16)
    kv_unit = min(n_kv, max(1, ATTN_LANES // (rep * BLOCK)))
    n_heads = n_kv * rep
    lse_rows = LANES_V7X // n_heads

    blocks = [(sq, j) for sq in range(nseq) for j in range(tq // BLOCK)]
    units = []
    for sq, j in blocks:
        cap = first_cap if j == 0 else band_cap
        cap = jnp.concatenate([cap] * (kv_unit * rep), axis=1)
        cur = slice(j * BLOCK, (j + 1) * BLOCK)
        if j == 0:
            k_prev, vt_prev = kp_ref[sq], vtp_ref[sq]
        else:
            prv = slice((j - 1) * BLOCK, j * BLOCK)
            k_prev, vt_prev = k_ref[sq, prv, :], vt_ref[sq, :, prv]
        kk = jnp.concatenate([k_prev, k_ref[sq, cur, :]], axis=0)
        vvt = jnp.concatenate([vt_prev, vt_ref[sq, :, cur]], axis=1)
        for g0 in range(0, n_kv, kv_unit):
            unit = range(g0, g0 + kv_unit)
            heads = [g * rep + r for g in unit for r in range(rep)]
            qbd = jnp.concatenate(
                [jnp.concatenate(
                    [jnp.concatenate([qt_ref[sq, (g * rep + r) * HEAD_DIM:(g * rep + r + 1) * HEAD_DIM, cur]
                                      for r in range(rep)], axis=1) if g2 == g else zero_q
                     for g2 in range(n_kv)], axis=0)
                 for g in unit], axis=1)
            st = jnp.minimum(_dot(kk, qbd), cap)
            units.append(((sq, j), g0, heads, st, vvt[g0 * HEAD_DIM:(g0 + kv_unit) * HEAD_DIM, :]))

    soft = []
    for _, g0, heads, st, vt_u in units:
        m = jnp.max(st, axis=0, keepdims=True)
        sink_p = None
        if use_sink:
            sk = jnp.concatenate([jnp.full((1, BLOCK), sink_ref[h] * LOG2_E, F32) for h in heads], axis=1)
            m = jnp.maximum(m, sk)
            sink_p = jnp.exp2(sk - m)
        soft.append((jnp.exp2(st - m).astype(BF16), m, sink_p))

    out_t = {blk: [] for blk in blocks}
    lse_t = {blk: [] for blk in blocks}
    ones_rows = jnp.ones((SUBLANES_V7X, 2 * BLOCK), BF16)
    for (blk, g0, heads, st, vt_u), (p, m, sink_p) in zip(units, soft):
        pv = _dot(jnp.concatenate([vt_u, ones_rows], axis=0), p)
        l = pv[kv_unit * HEAD_DIM:kv_unit * HEAD_DIM + 1, :]
        if use_sink:
            l = l + sink_p
        ot_all = pv[:kv_unit * HEAD_DIM, :] * (1.0 / l)
        lse = (m + jnp.log2(l)) * LN_2 if want_lse else None
        for n, h in enumerate(heads):
            g = h // rep
            out_t[blk].append(ot_all[(g - g0) * HEAD_DIM:(g - g0 + 1) * HEAD_DIM, n * BLOCK:(n + 1) * BLOCK])
            if want_lse:
                lse_t[blk].append(jnp.broadcast_to(lse[:, n * BLOCK:(n + 1) * BLOCK], (lse_rows, BLOCK)))
    for sq, j in blocks:
        cur = slice(j * BLOCK, (j + 1) * BLOCK)
        heads_t = out_t[sq, j]
        assert len(heads_t) == n_heads
        if want_lse:
            heads_t = [ot[part * lse_rows:(part + 1) * lse_rows] for part in range(HEAD_DIM // lse_rows)
                       for ot in heads_t]
            lse_ref[sq, cur, :] = jnp.concatenate(lse_t[sq, j], axis=0).T
        o_ref[sq, cur, :] = jnp.concatenate(heads_t, axis=0).T.astype(o_ref.dtype)


def _banded_attention(qt_arr, q_blk, k_arr, k_blk, vt_arr, v_blk, *, n_kv, rep, max_off, sinks=None,
                      want_lse=False, perm_step=0):
    n, seq_len, _ = k_arr.shape
    tq = min(ATTN_TILE, seq_len)
    nseq = ATTN_TILE // tq
    assert seq_len % tq == 0 and n % nseq == 0
    per = tq // BLOCK
    qw = n_kv * rep * HEAD_DIM
    kw = n_kv * HEAD_DIM
    prev_blk = lambda i: jnp.maximum(i * per - 1, 0)
    in_specs = [pl.BlockSpec((nseq, qw, tq), lambda b, i: (b, q_blk, i)),
                pl.BlockSpec((nseq, BLOCK, kw), lambda b, i: (b, prev_blk(i), k_blk)),
                pl.BlockSpec((nseq, tq, kw), lambda b, i: (b, i, k_blk)),
                pl.BlockSpec((nseq, kw, BLOCK), lambda b, i: (b, v_blk, prev_blk(i))),
                pl.BlockSpec((nseq, kw, tq), lambda b, i: (b, v_blk, i))]
    args = [qt_arr, k_arr, k_arr, vt_arr, vt_arr]
    if sinks is not None:
        in_specs.append(pl.BlockSpec(memory_space=pltpu.SMEM))
        args.append(sinks)
    out_spec = pl.BlockSpec((nseq, tq, qw), lambda b, i: (b, i, 0))
    out_shape = [jax.ShapeDtypeStruct((n, seq_len, qw), BF16)]
    out_specs = [out_spec]
    if want_lse:
        assert LANES_V7X % (n_kv * rep) == 0 and HEAD_DIM % (LANES_V7X // (n_kv * rep)) == 0
        out_shape.append(jax.ShapeDtypeStruct((n, seq_len, LANES_V7X), F32))
        out_specs.append(pl.BlockSpec((nseq, tq, LANES_V7X), lambda b, i: (b, i, 0)))
    body = functools.partial(_attn_body, n_kv=n_kv, rep=rep, max_off=max_off, tq=tq, nseq=nseq,
                             use_sink=sinks is not None, want_lse=want_lse, perm_step=perm_step)
    res = pl.pallas_call(
        body,
        grid=(n // nseq, seq_len // tq),
        in_specs=in_specs,
        out_specs=out_specs,
        out_shape=out_shape,
        compiler_params=_cparams(("parallel", "parallel"), 48),
        name="banded_attention",
    )(*args)
    return res if want_lse else res[0]


def _ssm_param_body(lr_ref, li_ref, ldt_ref, br_ref, bi_ref, ar_ref, ai_ref, bbr_ref, bbi_ref):
    lr, li = lr_ref[...], li_ref[...]
    dt = jnp.exp(ldt_ref[...])
    mag = jnp.exp(lr * dt)
    ab_re, ab_im = mag * jnp.cos(li * dt), mag * jnp.sin(li * dt)
    nr, ni = ab_re - 1.0, ab_im
    den = lr * lr + li * li
    f_re = (nr * lr + ni * li) / den
    f_im = (ni * lr - nr * li) / den
    br, bi = br_ref[...], bi_ref[...]
    ar_ref[...] = ab_re
    ai_ref[...] = ab_im
    bbr_ref[...] = f_re * br - f_im * bi
    bbi_ref[...] = f_re * bi + f_im * br


def _ssm_params(lam_re, lam_im, log_dt, b_re, b_im):
    nl = lam_re.shape[0]
    rows = nl * SSM_GROUPS * SSM_GROUP
    rep = lambda a: jnp.broadcast_to(a[:, :, None, :], (nl, SSM_GROUPS, SSM_GROUP, SSM_STATE)).reshape(
        rows, SSM_STATE)
    ldt = jnp.broadcast_to(log_dt[:, :, None, None], (nl, SSM_GROUPS, SSM_GROUP, SSM_STATE)).reshape(
        rows, SSM_STATE)
    bt = lambda b: jnp.swapaxes(b, 2, 3).reshape(rows, SSM_STATE)
    shape = jax.ShapeDtypeStruct((rows, SSM_STATE), F32)
    ar, ai, bbr, bbi = pl.pallas_call(
        _ssm_param_body, out_shape=[shape] * 4, name="ssm_params",
    )(rep(lam_re), rep(lam_im), ldt, bt(b_re), bt(b_im))
    pick = lambda a: a.reshape(nl, SSM_GROUPS, SSM_GROUP, SSM_STATE)[:, :, 0, :].reshape(nl, N_STATE)
    unflat = lambda a: a.reshape(nl, SSM_GROUPS, SSM_GROUP, SSM_STATE)
    return pick(ar), pick(ai), unflat(bbr), unflat(bbi)


SSM_HALF_GROUPS = SSM_GROUPS // 2
SSM_HALF_IN = SSM_HALF_GROUPS * SSM_GROUP
SSM_HALF_STATE = SSM_HALF_GROUPS * SSM_STATE


def _ssm_matrices(bb_re, bb_im, c_re, c_im):
    eye = jnp.eye(SSM_HALF_GROUPS, dtype=F32)

    def in_blocks(bb):
        bb = bb.reshape(2, SSM_HALF_GROUPS, SSM_GROUP, SSM_STATE)
        return jnp.einsum("ab,kahp->kahbp", eye, bb).reshape(2, SSM_HALF_IN, SSM_HALF_STATE)

    def out_blocks(c):
        c = c.reshape(2, SSM_HALF_GROUPS, SSM_GROUP, SSM_STATE)
        return jnp.einsum("ab,kahp->kapbh", eye, c).reshape(2, SSM_HALF_STATE, SSM_HALF_IN)

    b_mat = jnp.concatenate([in_blocks(bb_re), in_blocks(bb_im)], axis=2).astype(BF16)
    c_mat = jnp.concatenate([out_blocks(c_re), out_blocks(-c_im)], axis=1).astype(BF16)
    return b_mat, c_mat


def _ssm_body(u_ref, bm_ref, ar_ref, ai_ref, cm_ref, d_ref, wg_ref, bg_ref, o_ref,
              x_scr, sr_scr, si_scr, tb_scr):
    nb = SUBLANES_V7X
    width = 2 * SSM_HALF_STATE
    n_slab = SSM_WIDTH // LANES_V7X

    @pl.when(pl.program_id(0) == 0)
    def _():
        sr_scr[...] = jnp.zeros_like(sr_scr)
        si_scr[...] = jnp.zeros_like(si_scr)

    for b in range(nb):
        ub = u_ref[b].astype(F32)
        for s in range(n_slab):
            tb_scr[s, pl.ds(b, SSM_T, stride=nb), :] = ub[:, s * LANES_V7X:(s + 1) * LANES_V7X]
    u32 = jnp.concatenate([tb_scr[s] for s in range(n_slab)], axis=1)
    u = u32.astype(BF16)

    per_half = SSM_HALF_STATE // SSM_CHUNK
    chunks = [(half, c) for half in range(2) for c in range(per_half)]

    def cols_of(half, c):
        re0 = half * width + c * SSM_CHUNK
        return re0, re0 + SSM_HALF_STATE, half * SSM_HALF_STATE + c * SSM_CHUNK

    def input_map(half, c):
        re0, im0, _ = cols_of(half, c)
        uh = u[:, half * SSM_HALF_IN:(half + 1) * SSM_HALF_IN]
        for c0, b0 in ((re0, c * SSM_CHUNK), (im0, SSM_HALF_STATE + c * SSM_CHUNK)):
            x_scr[:, c0:c0 + SSM_CHUNK] = _dot(uh, bm_ref[half, :, b0:b0 + SSM_CHUNK])

    def recurrence(half, c):
        re0, im0, st0 = cols_of(half, c)
        a_re = ar_ref[:, st0:st0 + SSM_CHUNK]
        a_im = ai_ref[:, st0:st0 + SSM_CHUNK]
        xr, xi = sr_scr[:, st0:st0 + SSM_CHUNK], si_scr[:, st0:st0 + SSM_CHUNK]
        for t in range(SSM_T):
            rows = slice(t * nb, (t + 1) * nb)
            xr, xi = (a_re * xr - a_im * xi + x_scr[rows, re0:re0 + SSM_CHUNK],
                      a_re * xi + a_im * xr + x_scr[rows, im0:im0 + SSM_CHUNK])
            x_scr[rows, re0:re0 + SSM_CHUNK] = xr
            x_scr[rows, im0:im0 + SSM_CHUNK] = xi
        sr_scr[:, st0:st0 + SSM_CHUNK] = xr
        si_scr[:, st0:st0 + SSM_CHUNK] = xi

    def output_map(half, c):
        re0, im0, _ = cols_of(half, c)
        acc = None
        for c0, r0 in ((re0, c * SSM_CHUNK), (im0, SSM_HALF_STATE + c * SSM_CHUNK)):
            part = _dot(x_scr[:, c0:c0 + SSM_CHUNK].astype(BF16), cm_ref[half, r0:r0 + SSM_CHUNK, :])
            acc = part if acc is None else acc + part
        return acc

    ys = [None, None]
    n = len(chunks)
    for stage in range(n + 2):
        if stage < n:
            input_map(*chunks[stage])
        if 1 <= stage <= n:
            recurrence(*chunks[stage - 1])
        if stage >= 2:
            half, c = chunks[stage - 2]
            part = output_map(half, c)
            ys[half] = part if ys[half] is None else ys[half] + part
    y = jnp.concatenate(ys, axis=1) + d_ref[...] * u32
    z = jax.nn.gelu(y)
    z = z * jax.nn.sigmoid(_dot(z.astype(BF16), wg_ref[...]) + bg_ref[...])
    for s in range(n_slab):
        tb_scr[s] = z[:, s * LANES_V7X:(s + 1) * LANES_V7X]
    for b in range(nb):
        for s in range(n_slab):
            o_ref[b, :, s * LANES_V7X:(s + 1) * LANES_V7X] = (
                tb_scr[s, pl.ds(b, SSM_T, stride=nb), :].astype(o_ref.dtype))


def _ssm(u_arr, u_blk, b_mat, a_re, a_im, c_mat, d_skip, w_glu, b_glu):
    bsz, s, _ = u_arr.shape
    rows = SSM_T * SUBLANES_V7X
    return pl.pallas_call(
        _ssm_body,
        grid=(s // SSM_T,),
        in_specs=[pl.BlockSpec((bsz, SSM_T, SSM_WIDTH), lambda i: (0, i, u_blk)),
                  _resident(b_mat.shape), _resident(a_re.shape), _resident(a_im.shape),
                  _resident(c_mat.shape), _resident(d_skip.shape), _resident(w_glu.shape),
                  _resident(b_glu.shape)],
        out_specs=pl.BlockSpec((bsz, SSM_T, SSM_WIDTH), lambda i: (0, i, 0)),
        out_shape=jax.ShapeDtypeStruct((bsz, s, SSM_WIDTH), BF16),
        scratch_shapes=[pltpu.VMEM((rows, 2 * N_STATE), F32),
                        pltpu.VMEM((SUBLANES_V7X, N_STATE), F32),
                        pltpu.VMEM((SUBLANES_V7X, N_STATE), F32),
                        pltpu.VMEM((SSM_WIDTH // LANES_V7X, rows, LANES_V7X), F32)],
        compiler_params=_cparams(("arbitrary",), 40),
        name="s5_mixer",
    )(u_arr, b_mat, a_re, a_im, c_mat, d_skip, w_glu, b_glu)


def _merge_body(x_ref, ya_ref, o0_ref, o1_ref, o2_ref, l0_ref, l1_ref, l2_ref, yc_ref,
                gn_ref, wg_ref, wa_ref, wb_ref, wc_ref, wo_ref, out_ref, merged_scr, sub_scr, tok_scr):
    x = x_ref[...]
    h = _rmsnorm(x, gn_ref[...]).astype(BF16)
    n_slab = B_KV // LANES_V7X

    def token_order(slot, piece_of, width):
        for s in range(width // LANES_V7X):
            lanes = slice(s * LANES_V7X, (s + 1) * LANES_V7X)
            for r in range(DIL_WIDE):
                r4, m = r % DIL_MID, r // DIL_MID
                sub_scr[slot, s, r4, pl.ds(m, PERM_CHUNK, stride=DIL_RATIO), :] = piece_of(r)[:, lanes].astype(F32)
            for r4 in range(DIL_MID):
                tok_scr[slot, s, pl.ds(r4, BLOCK, stride=DIL_MID), :] = sub_scr[slot, s, r4]
        return [tok_scr[slot, s] for s in range(width // LANES_V7X)]

    mid = lambda ref: (lambda r: ref[r % DIL_MID, (r // DIL_MID) * PERM_CHUNK:(r // DIL_MID + 1) * PERM_CHUNK, :])
    wide = lambda ref: (lambda r: ref[r])
    o1, (l1,) = token_order(0, mid(o1_ref), B_KV), token_order(1, mid(l1_ref), LANES_V7X)
    o2, (l2,) = token_order(2, wide(o2_ref), B_KV), token_order(3, wide(l2_ref), LANES_V7X)
    l0 = l0_ref[...]
    top = jnp.maximum(jnp.maximum(l0, l1), l2)
    e0, e1, e2 = jnp.exp(l0 - top), jnp.exp(l1 - top), jnp.exp(l2 - top)
    inv = 1.0 / (e0 + e1 + e2)
    w0, w1, w2 = e0 * inv, e1 * inv, e2 * inv
    yb = jnp.concatenate(
        [w0 * o0_ref[:, s * LANES_V7X:(s + 1) * LANES_V7X].astype(F32) + w1 * o1[s] + w2 * o2[s]
         for s in range(n_slab)], axis=1).astype(BF16)
    ya = ya_ref[...]
    yc = yc_ref[...]
    step = 256
    for c in range(D_MODEL // step):
        cs = slice(c * step, (c + 1) * step)
        gate = lambda k: jax.nn.sigmoid(_dot(h, wg_ref[:, k * D_MODEL + c * step:k * D_MODEL + (c + 1) * step]))
        merged = (gate(0) * _dot(ya, wa_ref[:, cs]) + gate(1) * _dot(yb, wb_ref[:, cs])
                  + gate(2) * _dot(yc, wc_ref[:, cs]))
        merged_scr[:, cs] = merged.astype(BF16)
    out_ref[...] = x + _dot(merged_scr[...], wo_ref[...])


def _merge(x, ya, outs, lses, yc, gain, w_gate, w_a, w_b, w_c, w_out):
    bsz, s, _ = x.shape
    assert TOK_TILE == DIL_WIDE * PERM_CHUNK and TOK_TILE // DIL_MID == BLOCK
    tile = lambda width: pl.BlockSpec((None, TOK_TILE, width), lambda b, i: (b, i, 0))
    mid = lambda width: pl.BlockSpec((None, DIL_MID, BLOCK, width), lambda b, i: (b, 0, i, 0))
    wide = lambda width: pl.BlockSpec((None, DIL_WIDE, PERM_CHUNK, width), lambda b, i: (b, 0, i, 0))
    groups = lambda width: [tile(width), mid(width), wide(width)]
    n_slab = B_KV // LANES_V7X
    return pl.pallas_call(
        _merge_body,
        grid=(bsz, s // TOK_TILE),
        in_specs=[tile(D_MODEL), tile(A_Q)] + groups(B_KV) + groups(LANES_V7X) + [tile(SSM_WIDTH),
                  _resident(gain.shape), _resident(w_gate.shape), _resident(w_a.shape),
                  _resident(w_b.shape), _resident(w_c.shape), _resident(w_out.shape)],
        out_specs=tile(D_MODEL),
        out_shape=jax.ShapeDtypeStruct(x.shape, x.dtype),
        scratch_shapes=[pltpu.VMEM((TOK_TILE, D_MODEL), BF16),
                        pltpu.VMEM((4, n_slab, DIL_MID, BLOCK, LANES_V7X), F32),
                        pltpu.VMEM((4, n_slab, TOK_TILE, LANES_V7X), F32)],
        compiler_params=_cparams(("parallel", "parallel"), 48),
        name="gated_merge",
    )(x, ya, *outs, *lses, yc, gain, w_gate, w_a, w_b, w_c, w_out)


def _ffn_body(*refs, final):
    x_ref, gn_ref, wup_ref, cw_ref, cb_ref, wdn_ref = refs[:6]
    pos = 6
    gf_ref = None
    if final:
        gf_ref = refs[pos]
        pos += 1
    out_ref, carry_scr, work_scr, act_scr = refs[pos:pos + 4]
    pad = SUBLANES_V7X

    @pl.when(pl.program_id(1) == 0)
    def _():
        carry_scr[...] = jnp.zeros_like(carry_scr)

    x = x_ref[...]
    h = _rmsnorm(x, gn_ref[...]).astype(BF16)

    def conv_cols(slot, c0):
        cs = slice(c0, c0 + FFN_CHUNK)
        up = _dot(h, wup_ref[:, cs])
        work_scr[slot, 0:pad, :] = carry_scr[:, cs]
        work_scr[slot, pad:pad + FFN_TILE, :] = up
        carry_scr[:, cs] = up[FFN_TILE - pad:, :]
        acc = cb_ref[:, cs] + cw_ref[CONV_WIDTH - 1:CONV_WIDTH, cs] * up
        for k in range(CONV_WIDTH - 1):
            shift = CONV_WIDTH - 1 - k
            acc = acc + cw_ref[k:k + 1, cs] * work_scr[slot, pad - shift:pad - shift + FFN_TILE, :]
        return acc

    for c in range(FFN_DIM // FFN_CHUNK):
        gate = conv_cols(0, c * FFN_CHUNK)
        val = conv_cols(1, FFN_DIM + c * FFN_CHUNK)
        act_scr[:, c * FFN_CHUNK:(c + 1) * FFN_CHUNK] = (jax.nn.silu(gate) * val).astype(BF16)
    y = x + _dot(act_scr[...], wdn_ref[...])
    if final:
        y = _rmsnorm(y, gf_ref[...])
    out_ref[...] = y


def _ffn(x, gain, w_up, conv_w, conv_b, w_down, final_gain=None):
    bsz, s, _ = x.shape
    final = final_gain is not None
    tile = pl.BlockSpec((None, FFN_TILE, D_MODEL), lambda b, i: (b, i, 0))
    in_specs = [tile, _resident(gain.shape), _resident(w_up.shape), _resident(conv_w.shape),
                _resident(conv_b.shape), _resident(w_down.shape)]
    args = [x, gain, w_up, conv_w, conv_b, w_down]
    if final:
        in_specs.append(_resident(final_gain.shape))
        args.append(final_gain)
    return pl.pallas_call(
        functools.partial(_ffn_body, final=final),
        grid=(bsz, s // FFN_TILE),
        in_specs=in_specs,
        out_specs=tile,
        out_shape=jax.ShapeDtypeStruct(x.shape, x.dtype),
        scratch_shapes=[pltpu.VMEM((SUBLANES_V7X, 2 * FFN_DIM), F32),
                        pltpu.VMEM((2, SUBLANES_V7X + FFN_TILE, FFN_CHUNK), F32),
                        pltpu.VMEM((FFN_TILE, FFN_DIM), BF16)],
        compiler_params=_cparams(("arbitrary", "arbitrary"), 56),
        name="conv_ffn",
    )(*args)


def kernel(x, norm_mix, w_in, attn_sinks, ssm_lambda_re, ssm_lambda_im, ssm_log_dt, ssm_b_re, ssm_b_im,
           ssm_c_re, ssm_c_im, ssm_d, w_glu, b_glu, w_branch_a, w_branch_b, w_branch_c, w_out,
           norm_ffn, w_up, conv_w, conv_b, w_down, norm_final):
    bsz, s, _ = x.shape
    depth = w_in.shape[0]
    assert bsz == SUBLANES_V7X, "the S5 kernel maps the batch onto the sublane axis"
    assert s % TOK_TILE == 0 and s % PROJ_TILE == 0 and s % FFN_TILE == 0 and s % SSM_T == 0
    assert [d for _, d in DIL_PATTERNS] == [1, DIL_MID, DIL_WIDE]
    assert all(w // d == BLOCK for w, d in DIL_PATTERNS), "every dilated group spans one BLOCK of its sub-sequence"

    a_re, a_im, bb_re, bb_im = _ssm_params(ssm_lambda_re, ssm_lambda_im, ssm_log_dt, ssm_b_re, ssm_b_im)
    row = lambda v: v.reshape(1, -1)
    seqs = lambda a: a.reshape((a.shape[0] * a.shape[1],) + a.shape[2:])
    unseqs = lambda a: a.reshape((bsz, a.shape[0] // bsz) + a.shape[1:])
    dil_attn = functools.partial(_banded_attention, n_kv=DIL_HEADS, rep=1, max_off=BLOCK, want_lse=True)

    for l in range(depth):
        wl = w_in[l]
        cols_of = lambda off, width: wl[:, off:off + width]
        w_kd, w_vd = cols_of(IN_KD, B_KV), cols_of(IN_VD, B_KV)
        w_row = jnp.concatenate([cols_of(IN_U, SSM_WIDTH), w_kd, cols_of(IN_KA, A_KV)], axis=1).astype(BF16)
        w_col = jnp.concatenate([cols_of(IN_QA, A_Q), cols_of(IN_QD, B_KV), w_vd, cols_of(IN_VA, A_KV)],
                                axis=1).T.astype(BF16)
        w_mid_t = jnp.concatenate([cols_of(IN_QD + B_KV, B_KV), w_vd], axis=1).T.astype(BF16)
        w_wide_t = jnp.concatenate([cols_of(IN_QD + 2 * B_KV, B_KV), w_vd], axis=1).T.astype(BF16)
        w_gate = wl[:, W_MAIN:].astype(BF16)
        gain = row(norm_mix[l])
        rows, cols = _inproj(x, gain, w_row, w_col)
        k_mid, qv_mid, k_wide, qv_wide = _dil_proj(x, gain, w_kd.astype(BF16), w_mid_t, w_wide_t)

        ya = _banded_attention(cols, COL_QA // A_Q, rows, ROW_KA // A_KV, cols, COL_VA // A_KV,
                               n_kv=SWA_KV_HEADS, rep=SWA_Q_HEADS // SWA_KV_HEADS,
                               max_off=SWA_WINDOW - 1, sinks=attn_sinks[l])

        o0, l0 = dil_attn(cols, COL_QD // B_KV, rows, ROW_KD // B_KV, cols, COL_VD // B_KV)
        o1, l1 = dil_attn(seqs(qv_mid), 0, seqs(k_mid), 0, seqs(qv_mid), 1, perm_step=DIL_RATIO)
        o2, l2 = dil_attn(seqs(qv_wide), 0, seqs(k_wide), 0, seqs(qv_wide), 1)
        outs = [o0, unseqs(o1), unseqs(o2)]
        lses = [l0, unseqs(l1), unseqs(l2)]

        b_mat, c_mat = _ssm_matrices(bb_re[l], bb_im[l], ssm_c_re[l], ssm_c_im[l])
        bcast = lambda v: jnp.broadcast_to(v[None, :], (SUBLANES_V7X, N_STATE))
        yc = _ssm(rows, ROW_U // SSM_WIDTH, b_mat, bcast(a_re[l]), bcast(a_im[l]), c_mat, row(ssm_d[l]),
                  w_glu[l].astype(BF16), row(b_glu[l]))

        lane_w = LANES_V7X // DIL_HEADS
        w_b = w_branch_b[l].reshape(DIL_HEADS, HEAD_DIM // lane_w, lane_w, D_MODEL).transpose(1, 0, 2, 3)
        x = _merge(x, ya, outs, lses, yc, row(norm_mix[l]), w_gate, w_branch_a[l].astype(BF16),
                   w_b.reshape(B_KV, D_MODEL).astype(BF16), w_branch_c[l].astype(BF16), w_out[l].astype(BF16))
        x = _ffn(x, row(norm_ffn[l]), w_up[l].astype(BF16), conv_w[l], row(conv_b[l]),
                 w_down[l].astype(BF16), final_gain=row(norm_final) if l == depth - 1 else None)
    return x
```
